```python
import jax, jax.numpy as jnp
from jax import lax
import numpy as np

D_MODEL = 2048
BATCH = 2
SEQ = 8192
DEPTH = 4

GRID_W = 64
CTX_LEN = 256
HEAD_DIM = 128
ATTN_SCALE = HEAD_DIM ** -0.5
ROPE_THETA = 10000.0
NEG_INF = -1e30

A_Q_HEADS = 8
A_KV_HEADS = 2
A_GROUP = A_Q_HEADS // A_KV_HEADS
A_WINDOW = 128
A_BLOCK = 128
B_HEADS = 8
B_WIN_ROWS = 8
B_WIN_COLS = 16
B_QCOLS = 16
B_KCOLS = 32
C_Q_HEADS = 16
C_KV_HEADS = 4
C_GROUP = C_Q_HEADS // C_KV_HEADS
C_BLOCK = 128

MIX_WIDTH = (A_Q_HEADS + B_HEADS) * HEAD_DIM
AB_SIZES = (A_Q_HEADS * HEAD_DIM, A_KV_HEADS * HEAD_DIM, A_KV_HEADS * HEAD_DIM,
            B_HEADS * HEAD_DIM, B_HEADS * HEAD_DIM, B_HEADS * HEAD_DIM)
AB_IN = sum(AB_SIZES)
C_SIZES = (C_Q_HEADS * HEAD_DIM, C_KV_HEADS * HEAD_DIM, C_KV_HEADS * HEAD_DIM)
C_IN = sum(C_SIZES)

PEER_HEADS = 8
PEER_N_KEYS = 128
PEER_N_EXPERTS = PEER_N_KEYS ** 2
PEER_KEY_DIM = 128
PEER_TOPK = 16
PEER_CHUNK = 128

N_MOD = 6
DEEPNORM_ALPHA = (2 * DEPTH) ** 0.25
DEEPNORM_BETA = (8 * DEPTH) ** -0.25
N_EVEN = (DEPTH + 1) // 2
N_ODD = DEPTH // 2

kernel_name = 'hybrid_dit_window_natten_qknorm_peer'


def split_points(sizes):
    return [int(s) for s in np.cumsum(sizes)[:-1]]


def heads(t, n):
    return t.reshape(t.shape[:2] + (n, HEAD_DIM))


def layer_norm(x, g, b, eps=1e-5):
    xf = x.astype(jnp.float32)
    mu = jnp.mean(xf, -1, keepdims=True)
    var = jnp.mean(jnp.square(xf - mu), -1, keepdims=True)
    return ((xf - mu) * lax.rsqrt(var + eps)).astype(x.dtype) * g + b


def rms_norm(x, g, eps=1e-6):
    xf = x.astype(jnp.float32)
    return (xf * lax.rsqrt(jnp.mean(jnp.square(xf), -1, keepdims=True) + eps)).astype(x.dtype) * g


def axial_rope(n_tokens):
    t = jnp.arange(n_tokens, dtype=jnp.int32)
    row = (t // GRID_W).astype(jnp.float32)
    col = (t % GRID_W).astype(jnp.float32)
    n_freq = HEAD_DIM // 4
    inv_freq = ROPE_THETA ** (-jnp.arange(n_freq, dtype=jnp.float32) / n_freq)
    ang_r = row[:, None] * inv_freq[None, :]
    ang_c = col[:, None] * inv_freq[None, :]
    ang = jnp.concatenate([ang_r, ang_r, ang_c, ang_c], axis=-1)[:, None, :]
    return jnp.cos(ang), jnp.sin(ang)


def apply_rope(x, cos, sin):
    x1, x2, x3, x4 = jnp.split(x, 4, axis=-1)
    rot = jnp.concatenate([-x2, x1, -x4, x3], axis=-1)
    return (x.astype(jnp.float32) * cos + rot.astype(jnp.float32) * sin).astype(x.dtype)


def context_attention(q, k, v, sink=None):
    B, C, hkv, g = q.shape[:4]
    s = jnp.einsum('bqhgd,bkhd->bhgqk', q, k).astype(jnp.float32) * ATTN_SCALE
    if sink is not None:
        s_sink = jnp.broadcast_to(sink.astype(jnp.float32).reshape(1, hkv, g, 1, 1), s.shape[:-1] + (1,))
        s = jnp.concatenate([s, s_sink], axis=-1)
    p = jax.nn.softmax(s, axis=-1)
    if sink is not None:
        p = p[..., :-1]
    out = jnp.einsum('bhgqk,bkhd->bqhgd', p.astype(v.dtype), v)
    return out.reshape(B, C, hkv * g * HEAD_DIM)


def window_attention(q, k, v, k_ctx, v_ctx, sink):
    B, L = q.shape[:2]
    nb = L // A_BLOCK
    qb = q.reshape(B, nb, A_BLOCK, A_KV_HEADS, A_GROUP, HEAD_DIM)

    def band(t):
        tp = jnp.pad(t, ((0, 0), (A_BLOCK, A_BLOCK), (0, 0), (0, 0)))
        tp = tp.reshape(B, nb + 2, A_BLOCK, A_KV_HEADS, HEAD_DIM)
        return jnp.concatenate([tp[:, :-2], tp[:, 1:-1], tp[:, 2:]], axis=2)

    kb, vb = band(k), band(v)
    s_loc = jnp.einsum('bnqhgd,bnkhd->bnhgqk', qb, kb).astype(jnp.float32) * ATTN_SCALE
    s_ctx = jnp.einsum('bnqhgd,bchd->bnhgqc', qb, k_ctx).astype(jnp.float32) * ATTN_SCALE
    qi = np.arange(A_BLOCK)[:, None]
    kk = np.arange(3 * A_BLOCK)[None, :]
    in_window = np.abs(kk - A_BLOCK - qi) <= A_WINDOW
    kpos = (np.arange(nb)[:, None] - 1) * A_BLOCK + np.arange(3 * A_BLOCK)[None, :]
    in_range = (kpos >= 0) & (kpos < L)
    mask = in_window[None] & in_range[:, None, :]
    s_loc = jnp.where(mask[None, :, None, None], s_loc, NEG_INF)
    s_sink = jnp.broadcast_to(sink.astype(jnp.float32).reshape(1, 1, A_KV_HEADS, A_GROUP, 1, 1),
                              s_loc.shape[:-1] + (1,))
    p = jax.nn.softmax(jnp.concatenate([s_loc, s_ctx, s_sink], axis=-1), axis=-1).astype(v.dtype)
    n_loc = 3 * A_BLOCK
    n_ctx = k_ctx.shape[1]
    out = (jnp.einsum('bnhgqk,bnkhd->bnqhgd', p[..., :n_loc], vb)
           + jnp.einsum('bnhgqc,bchd->bnqhgd', p[..., n_loc:n_loc + n_ctx], v_ctx))
    return out.reshape(B, L, A_Q_HEADS * HEAD_DIM)


def neighborhood_attention(q, k, v, k_ctx, v_ctx, rpb):
    B, L = q.shape[:2]
    rows = L // GRID_W
    kr = min(B_WIN_ROWS, rows)
    n_cb = GRID_W // B_QCOLS
    qcol = np.arange(GRID_W).reshape(n_cb, B_QCOLS)
    kstart = np.clip(np.arange(n_cb) * B_QCOLS - B_WIN_COLS // 2, 0, GRID_W - B_KCOLS)
    kcol = kstart[:, None] + np.arange(B_KCOLS)[None, :]
    wstart = np.clip(qcol - B_WIN_COLS // 2, 0, GRID_W - B_WIN_COLS)
    col_ok = (kcol[:, None, :] >= wstart[:, :, None]) & (kcol[:, None, :] < wstart[:, :, None] + B_WIN_COLS)
    dcol = np.clip(kcol[:, None, :] - qcol[:, :, None] + B_WIN_COLS - 1, 0, 2 * B_WIN_COLS - 2)
    bias_c = jnp.transpose(rpb[:, :, dcol], (0, 2, 3, 1, 4)).astype(jnp.float32)
    bias_c = jnp.where(col_ok[None, :, :, None, :], bias_c, NEG_INF)
    rstart = np.clip(np.arange(rows) - kr // 2, 0, rows - kr)
    drow = rstart[:, None] + np.arange(kr)[None, :] - np.arange(rows)[:, None] + B_WIN_ROWS - 1
    k_g = k.reshape(B, rows, GRID_W, B_HEADS, HEAD_DIM)
    v_g = v.reshape(B, rows, GRID_W, B_HEADS, HEAD_DIM)
    q_rows = jnp.moveaxis(q.reshape(B, rows, n_cb, B_QCOLS, B_HEADS, HEAD_DIM), 1, 0)
    n_loc = kr * B_KCOLS

    def row_block(args):
        q_r, r0, dr = args
        k_r = lax.dynamic_slice_in_dim(k_g, r0, kr, axis=1)[:, :, kcol]
        v_r = lax.dynamic_slice_in_dim(v_g, r0, kr, axis=1)[:, :, kcol]
        s_loc = jnp.einsum('bnqhd,brnkhd->bhnqrk', q_r, k_r).astype(jnp.float32) * ATTN_SCALE
        s_loc = s_loc + bias_c[:, :, :, dr][None]
        s_loc = s_loc.reshape(B, B_HEADS, n_cb, B_QCOLS, n_loc)
        s_ctx = jnp.einsum('bnqhd,bchd->bhnqc', q_r, k_ctx).astype(jnp.float32) * ATTN_SCALE
        p = jax.nn.softmax(jnp.concatenate([s_loc, s_ctx], axis=-1), axis=-1).astype(v.dtype)
        p_loc = p[..., :n_loc].reshape(B, B_HEADS, n_cb, B_QCOLS, kr, B_KCOLS)
        return (jnp.einsum('bhnqrk,brnkhd->bnqhd', p_loc, v_r)
                + jnp.einsum('bhnqc,bchd->bnqhd', p[..., n_loc:], v_ctx))

    out = lax.map(row_block, (q_rows, jnp.asarray(rstart, jnp.int32), jnp.asarray(drow, jnp.int32)))
    return jnp.moveaxis(out, 0, 1).reshape(B, L, B_HEADS * HEAD_DIM)


def block_dense_attention(q, k, v):
    B, L = q.shape[:2]
    nb = L // C_BLOCK
    qb = jnp.moveaxis(q.reshape(B, nb, C_BLOCK, C_KV_HEADS, C_GROUP, HEAD_DIM), 1, 0)

    def one_block(q_i):
        s = jnp.einsum('bqhgd,bshd->bhgqs', q_i, k).astype(jnp.float32) * ATTN_SCALE
        p = jax.nn.softmax(s, axis=-1).astype(v.dtype)
        return jnp.einsum('bhgqs,bshd->bqhgd', p, v)

    out = lax.map(one_block, qb)
    return jnp.moveaxis(out, 0, 1).reshape(B, L, C_Q_HEADS * HEAD_DIM)


def mixer_ab(h_lat, h_ctx, w_in, w_out, sink, rpb, cos, sin, ctx_out):
    B, L = h_lat.shape[:2]
    C = h_ctx.shape[1]
    pts = split_points(AB_SIZES)
    qa, ka, va, qb, kb, vb = jnp.split(h_lat @ w_in, pts, axis=-1)
    qa_c, ka_c, va_c, qb_c, kb_c, vb_c = jnp.split(h_ctx @ w_in, pts, axis=-1)
    ka_c, va_c = heads(ka_c, A_KV_HEADS), heads(va_c, A_KV_HEADS)
    kb_c, vb_c = heads(kb_c, B_HEADS), heads(vb_c, B_HEADS)
    qa = apply_rope(heads(qa, A_Q_HEADS), cos, sin).reshape(B, L, A_KV_HEADS, A_GROUP, HEAD_DIM)
    ka = apply_rope(heads(ka, A_KV_HEADS), cos, sin)
    out_a = window_attention(qa, ka, heads(va, A_KV_HEADS), ka_c, va_c, sink)
    out_b = neighborhood_attention(heads(qb, B_HEADS), heads(kb, B_HEADS), heads(vb, B_HEADS), kb_c, vb_c, rpb)
    y_lat = jnp.concatenate([out_a, out_b], axis=-1) @ w_out
    y_ctx = None
    if ctx_out:
        oa_c = context_attention(heads(qa_c, A_Q_HEADS).reshape(B, C, A_KV_HEADS, A_GROUP, HEAD_DIM), ka_c, va_c, sink)
        ob_c = context_attention(heads(qb_c, B_HEADS)[:, :, :, None, :], kb_c, vb_c)
        y_ctx = jnp.concatenate([oa_c, ob_c], axis=-1) @ w_out
    return y_lat, y_ctx


def mixer_c(h_lat, h_ctx, w_in, w_out, q_gain, k_gain, cos, sin, ctx_out):
    B, L = h_lat.shape[:2]
    C = h_ctx.shape[1]
    pts = split_points(C_SIZES)

    def qkv(h):
        q, k, v = jnp.split(h @ w_in, pts, axis=-1)
        return (rms_norm(heads(q, C_Q_HEADS), q_gain), rms_norm(heads(k, C_KV_HEADS), k_gain), heads(v, C_KV_HEADS))

    q, k, v = qkv(h_lat)
    q_c, k_c, v_c = qkv(h_ctx)
    q = apply_rope(q, cos, sin).reshape(B, L, C_KV_HEADS, C_GROUP, HEAD_DIM)
    k = apply_rope(k, cos, sin)
    k_all = jnp.concatenate([k_c, k], axis=1)
    v_all = jnp.concatenate([v_c, v], axis=1)
    y_lat = block_dense_attention(q, k_all, v_all) @ w_out
    y_ctx = None
    if ctx_out:
        y_ctx = context_attention(q_c.reshape(B, C, C_KV_HEADS, C_GROUP, HEAD_DIM), k_c, v_c) @ w_out
    return y_lat, y_ctx


def peer(h, wq, subkeys, u, v):
    T, D = h.shape
    hc = h.reshape(T // PEER_CHUNK, PEER_CHUNK, D)

    def chunk(xi):
        q = (xi @ wq).reshape(PEER_CHUNK, PEER_HEADS, 2, PEER_KEY_DIM)
        s = jnp.einsum('thpk,hpnk->thpn', q, subkeys).astype(jnp.float32)
        s_top, i_top = lax.top_k(s, PEER_TOPK)
        cand = s_top[:, :, 0, :, None] + s_top[:, :, 1, None, :]
        cand_idx = i_top[:, :, 0, :, None] * PEER_N_KEYS + i_top[:, :, 1, None, :]
        best, pos = lax.top_k(cand.reshape(PEER_CHUNK, PEER_HEADS, PEER_TOPK * PEER_TOPK), PEER_TOPK)
        idx = jnp.take_along_axis(cand_idx.reshape(PEER_CHUNK, PEER_HEADS, PEER_TOPK * PEER_TOPK), pos, axis=-1)
        g = jax.nn.softmax(best, axis=-1)
        act = jax.nn.gelu(jnp.einsum('thkd,td->thk', u[idx], xi).astype(jnp.float32), approximate=False)
        return jnp.einsum('thk,thkd->td', (g * act).astype(v.dtype), v[idx])

    return lax.map(chunk, hc).reshape(T, D)


def setup_inputs(seed: int = 0) -> dict:
    key = jax.random.key(seed)
    ks = jax.random.split(key, 20)
    D = D_MODEL

    def nrm(k, shape, s):
        return jax.random.normal(k, shape, jnp.float32) * s

    return {
        'x': nrm(ks[0], (BATCH, SEQ, D), 1.0),
        'c': nrm(ks[1], (BATCH, D), 1.0),
        'ctx': nrm(ks[2], (BATCH, CTX_LEN, D), 1.0),
        'c_ctx': nrm(ks[3], (D,), 1.0),
        'mod_w': nrm(ks[4], (DEPTH, D, N_MOD * D), 0.5 * D ** -0.5),
        'mod_b': nrm(ks[5], (DEPTH, N_MOD * D), 0.02),
        'ln_g': 1.0 + nrm(ks[6], (DEPTH, 2, D), 0.05),
        'ln_b': nrm(ks[7], (DEPTH, 2, D), 0.02),
        'ab_w_in': nrm(ks[8], (N_EVEN, D, AB_IN), D ** -0.5),
        'ab_w_out': nrm(ks[9], (N_EVEN, MIX_WIDTH, D), DEEPNORM_BETA * MIX_WIDTH ** -0.5),
        'a_sink': nrm(ks[10], (N_EVEN, A_Q_HEADS), 0.5),
        'b_rpb': nrm(ks[11], (N_EVEN, B_HEADS, 2 * B_WIN_ROWS - 1, 2 * B_WIN_COLS - 1), 0.1),
        'c_w_in': nrm(ks[12], (N_ODD, D, C_IN), D ** -0.5),
        'c_w_out': nrm(ks[13], (N_ODD, MIX_WIDTH, D), DEEPNORM_BETA * MIX_WIDTH ** -0.5),
        'c_q_gain': 1.0 + nrm(ks[14], (N_ODD, HEAD_DIM), 0.05),
        'c_k_gain': 1.0 + nrm(ks[15], (N_ODD, HEAD_DIM), 0.05),
        'peer_wq': nrm(ks[16], (DEPTH, D, PEER_HEADS * 2 * PEER_KEY_DIM), D ** -0.5),
        'peer_subkeys': nrm(ks[17], (DEPTH, PEER_HEADS, 2, PEER_N_KEYS, PEER_KEY_DIM), PEER_KEY_DIM ** -0.5),
        'peer_u': nrm(ks[18], (DEPTH, PEER_N_EXPERTS, D), D ** -0.5),
        'peer_v': nrm(ks[19], (DEPTH, PEER_N_EXPERTS, D), DEEPNORM_BETA),
    }


def reference(x, c, ctx, c_ctx, mod_w, mod_b, ln_g, ln_b, ab_w_in, ab_w_out, a_sink, b_rpb,
              c_w_in, c_w_out, c_q_gain, c_k_gain, peer_wq, peer_subkeys, peer_u, peer_v):
    B, L, D = x.shape
    C = ctx.shape[1]
    cos, sin = axial_rope(L)
    xc = ctx
    for layer in range(DEPTH):
        last = layer == DEPTH - 1
        i = layer // 2
        mod_lat = (jax.nn.silu(c) @ mod_w[layer] + mod_b[layer])[:, None, :]
        mod_ctx = jax.nn.silu(c_ctx) @ mod_w[layer] + mod_b[layer]
        sh1, sc1, g1, sh2, sc2, g2 = jnp.split(mod_lat, N_MOD, axis=-1)
        csh1, csc1, cg1, csh2, csc2, cg2 = jnp.split(mod_ctx, N_MOD, axis=-1)
        h_lat = x * (1 + sc1) + sh1
        h_ctx = xc * (1 + csc1) + csh1
        if layer % 2 == 0:
            y_lat, y_ctx = mixer_ab(h_lat, h_ctx, ab_w_in[i], ab_w_out[i], a_sink[i], b_rpb[i], cos, sin, not last)
        else:
            y_lat, y_ctx = mixer_c(h_lat, h_ctx, c_w_in[i], c_w_out[i], c_q_gain[i], c_k_gain[i], cos, sin, not last)
        x = layer_norm(DEEPNORM_ALPHA * x + g1 * y_lat, ln_g[layer, 0], ln_b[layer, 0])
        h_lat = x * (1 + sc2) + sh2
        if last:
            y = peer(h_lat.reshape(B * L, D), peer_wq[layer], peer_subkeys[layer], peer_u[layer], peer_v[layer])
            x = layer_norm(DEEPNORM_ALPHA * x + g2 * y.reshape(B, L, D), ln_g[layer, 1], ln_b[layer, 1])
        else:
            xc = layer_norm(DEEPNORM_ALPHA * xc + cg1 * y_ctx, ln_g[layer, 0], ln_b[layer, 0])
            h_ctx = xc * (1 + csc2) + csh2
            tokens = jnp.concatenate([h_lat.reshape(B * L, D), h_ctx.reshape(B * C, D)], axis=0)
            y = peer(tokens, peer_wq[layer], peer_subkeys[layer], peer_u[layer], peer_v[layer])
            x = layer_norm(DEEPNORM_ALPHA * x + g2 * y[:B * L].reshape(B, L, D), ln_g[layer, 1], ln_b[layer, 1])
            xc = layer_norm(DEEPNORM_ALPHA * xc + cg2 * y[B * L:].reshape(B, C, D), ln_g[layer, 1], ln_b[layer, 1])
    return x
```

```python
import functools

import numpy as np
import jax
import jax.numpy as jnp
from jax import lax
from jax.experimental import pallas as pl
from jax.experimental.pallas import tpu as pltpu

F32 = jnp.float32
BF16 = jnp.bfloat16

DEPTH = 4
GRID_W = 64
HEAD_DIM = 128
ATTN_SCALE = HEAD_DIM ** -0.5
ROPE_THETA = 10000.0
NEG_INF = -1e30

A_Q_HEADS = 8
A_KV_HEADS = 2
A_GROUP = A_Q_HEADS // A_KV_HEADS
A_WINDOW = 128
B_HEADS = 8
B_WIN_ROWS = 8
B_WIN_COLS = 16
C_Q_HEADS = 16
C_KV_HEADS = 4
C_GROUP = C_Q_HEADS // C_KV_HEADS

PEER_HEADS = 8
PEER_N_KEYS = 128
PEER_TOPK = 16

DEEPNORM_ALPHA = (2 * DEPTH) ** 0.25

VMEM_LIMIT_BYTES = 56 * 1024 * 1024

ROW_TILE = 512
COL_TILE = 512
FLAG_GROUP = 256


def _dot(a, b):
    return jnp.dot(a, b, preferred_element_type=F32)


def _dot_nt(a, b):
    return lax.dot_general(a, b, (((1,), (1,)), ((), ())), preferred_element_type=F32)


def _params(sem, vmem=None):
    return pltpu.CompilerParams(dimension_semantics=sem, vmem_limit_bytes=vmem or VMEM_LIMIT_BYTES)


def _mod_kernel(c_ref, w_ref, b_ref, o_ref):
    c = c_ref[...]
    a = c * (1.0 / (1.0 + jnp.exp(-c)))
    o_ref[0] = _dot(a.astype(BF16), w_ref[0].astype(BF16)) + b_ref[0]


def _modulation(c_rows, mod_w, mod_b):
    n_layers, d, n = mod_w.shape
    tn = 1024
    rows = c_rows.shape[0]
    return pl.pallas_call(
        _mod_kernel,
        grid=(n_layers, n // tn),
        in_specs=[
            pl.BlockSpec((rows, d), lambda l, j: (0, 0)),
            pl.BlockSpec((1, d, tn), lambda l, j: (l, 0, j)),
            pl.BlockSpec((1, 1, tn), lambda l, j: (l, 0, j)),
        ],
        out_specs=pl.BlockSpec((1, rows, tn), lambda l, j: (l, 0, j)),
        out_shape=jax.ShapeDtypeStruct((n_layers, rows, n), F32),
        compiler_params=_params(("arbitrary", "arbitrary")),
    )(c_rows, mod_w, mod_b.reshape(n_layers, 1, n))


FLAG_RMS_Q = 1
FLAG_RMS_K = 2
FLAG_ROPE = 4
FLAG_SCALE = 8


def _proj_kernel(flags_ref, x_ref, sc_ref, sh_ref, w_ref, cos_ref, sin_ref, gain_ref, o_ref, *rest,
                 n_lat_tiles, emit_h):
    if emit_h:
        h_out_ref, h_scr, acc_scr = rest
    else:
        h_scr, acc_scr = rest
    i = pl.program_id(0)
    j = pl.program_id(1)
    tn = o_ref.shape[1]

    @pl.when(j == 0)
    def _():
        h = (x_ref[...] * (1.0 + sc_ref[0]) + sh_ref[0]).astype(BF16)
        h_scr[...] = h
        if emit_h:
            h_out_ref[...] = h

    acc_scr[...] = _dot(h_scr[...], w_ref[...])
    is_lat = i < n_lat_tiles
    lane = lax.broadcasted_iota(jnp.int32, (1, HEAD_DIM), 1)
    odd_seg = ((lane // (HEAD_DIM // 4)) % 2) == 1

    for g in range(tn // FLAG_GROUP):
        fl = flags_ref[j * (tn // FLAG_GROUP) + g]
        rms = fl & 3
        do_rope = jnp.logical_and((fl & FLAG_ROPE) != 0, is_lat)
        scale = jnp.where((fl & FLAG_SCALE) != 0, ATTN_SCALE, 1.0).astype(F32)
        for hh in range(FLAG_GROUP // HEAD_DIM):
            c0 = g * FLAG_GROUP + hh * HEAD_DIM
            cols = slice(c0, c0 + HEAD_DIM)

            @pl.when(rms != 0)
            def _():
                y = acc_scr[:, cols]
                gain = gain_ref[pl.ds(rms - 1, 1), :]
                y = y * lax.rsqrt(jnp.mean(y * y, axis=-1, keepdims=True) + 1e-6)
                acc_scr[:, cols] = y * gain

            @pl.when(do_rope)
            def _():
                y = acc_scr[:, cols]
                r_dn = pltpu.roll(y, HEAD_DIM // 4, 1)
                r_up = pltpu.roll(y, 3 * HEAD_DIM // 4, 1)
                y = y * cos_ref[...] + jnp.where(odd_seg, r_dn, r_up) * sin_ref[...]
                o_ref[:, cols] = (y * scale).astype(o_ref.dtype)

            @pl.when(jnp.logical_not(do_rope))
            def _():
                o_ref[:, cols] = (acc_scr[:, cols] * scale).astype(o_ref.dtype)


def _projection(x, sc, sh, w, flags, cos, sin_signed, gains, *, seq_len, n_lat_rows, out_dtype, emit_h=False):
    t, d = x.shape
    n = w.shape[1]
    tm, tn = ROW_TILE, COL_TILE
    n_lat_tiles = n_lat_rows // tm
    pos_tiles = seq_len // tm
    n_batch = n_lat_rows // seq_len

    def mod_map(i, j, fl):
        return (jnp.minimum((i * tm) // seq_len, n_batch), 0, 0)

    out_shape = [jax.ShapeDtypeStruct((t, n), out_dtype)]
    out_specs = [pl.BlockSpec((tm, tn), lambda i, j, fl: (i, j))]
    if emit_h:
        out_shape.append(jax.ShapeDtypeStruct((t, d), BF16))
        out_specs.append(pl.BlockSpec((tm, d), lambda i, j, fl: (i, 0)))
    grid_spec = pltpu.PrefetchScalarGridSpec(
        num_scalar_prefetch=1,
        grid=(t // tm, n // tn),
        in_specs=[
            pl.BlockSpec((tm, d), lambda i, j, fl: (i, 0)),
            pl.BlockSpec((1, 1, d), mod_map),
            pl.BlockSpec((1, 1, d), mod_map),
            pl.BlockSpec((d, tn), lambda i, j, fl: (0, j)),
            pl.BlockSpec((tm, HEAD_DIM), lambda i, j, fl: (i % pos_tiles, 0)),
            pl.BlockSpec((tm, HEAD_DIM), lambda i, j, fl: (i % pos_tiles, 0)),
            pl.BlockSpec((8, HEAD_DIM), lambda i, j, fl: (0, 0)),
        ],
        out_specs=out_specs,
        scratch_shapes=[pltpu.VMEM((tm, d), BF16), pltpu.VMEM((tm, tn), F32)],
    )
    res = pl.pallas_call(
        functools.partial(_proj_kernel, n_lat_tiles=n_lat_tiles, emit_h=emit_h),
        grid_spec=grid_spec,
        out_shape=out_shape,
        compiler_params=_params(("arbitrary", "arbitrary")),
    )(flags, x, sc, sh, w, cos, sin_signed, gains)
    return res if emit_h else res[0]


def _flash_kernel(sink_ref, q_ref, kc_ref, vc_ref, *rest, group, tk, n_chunks, has_sink):
    if n_chunks:
        kl_ref, vl_ref, o_ref, m_scr, l_scr, acc_scr = rest
    else:
        o_ref, m_scr, l_scr, acc_scr = rest
    tq = q_ref.shape[0]
    q = jnp.concatenate([q_ref[:, hh * HEAD_DIM:(hh + 1) * HEAD_DIM] for hh in range(group)], axis=0)

    s = _dot_nt(q, kc_ref[...])
    m = jnp.max(s, axis=-1, keepdims=True)
    p = jnp.exp(s - m)
    m_scr[...] = m
    l_scr[...] = jnp.sum(p, axis=-1, keepdims=True)
    acc_scr[...] = _dot(p.astype(BF16), vc_ref[...])

    if n_chunks:
        def body(c, carry):
            r0 = pl.multiple_of(c * tk, tk)
            s = _dot_nt(q, kl_ref[pl.ds(r0, tk), :])
            m_old = m_scr[...]
            m_new = jnp.maximum(m_old, jnp.max(s, axis=-1, keepdims=True))
            a = jnp.exp(m_old - m_new)
            p = jnp.exp(s - m_new)
            l_scr[...] = a * l_scr[...] + jnp.sum(p, axis=-1, keepdims=True)
            acc_scr[...] = a * acc_scr[...] + _dot(p.astype(BF16), vl_ref[pl.ds(r0, tk), :])
            m_scr[...] = m_new
            return carry

        lax.fori_loop(0, n_chunks, body, 0)

    m = m_scr[...]
    l = l_scr[...]
    acc = acc_scr[...]
    if has_sink:
        kvh = pl.program_id(1)
        sk = jnp.concatenate(
            [jnp.full((tq, 1), sink_ref[kvh * group + hh], F32) for hh in range(group)], axis=0)
        m_new = jnp.maximum(m, sk)
        a = jnp.exp(m - m_new)
        l = a * l + jnp.exp(sk - m_new)
        acc = a * acc
    o = acc / l
    for hh in range(group):
        o_ref[:, hh * HEAD_DIM:(hh + 1) * HEAD_DIM] = o[hh * tq:(hh + 1) * tq].astype(o_ref.dtype)


def _flash(p, sink, *, n_batch, q_rows, q_row0, q_col0, kv_heads, group, k_col0, v_col0,
           ctx_row0, ctx_len, lat_len, tq, tk=512):
    nq = q_rows // tq
    n_chunks = lat_len // tk if lat_len else 0
    has_sink = sink is not None
    if sink is None:
        sink = jnp.zeros((kv_heads * group,), F32)
    gw = group * HEAD_DIM
    kb, vb, qb = k_col0 // HEAD_DIM, v_col0 // HEAD_DIM, q_col0 // gw
    q0b, c0b = q_row0 // tq, ctx_row0 // ctx_len
    in_specs = [
        pl.BlockSpec(memory_space=pltpu.SMEM),
        pl.BlockSpec((tq, gw), lambda b, h, i: (q0b + b * nq + i, qb + h)),
        pl.BlockSpec((ctx_len, HEAD_DIM), lambda b, h, i: (c0b + b, kb + h)),
        pl.BlockSpec((ctx_len, HEAD_DIM), lambda b, h, i: (c0b + b, vb + h)),
    ]
    args = [sink, p, p, p]
    if n_chunks:
        in_specs += [
            pl.BlockSpec((lat_len, HEAD_DIM), lambda b, h, i: (b, kb + h)),
            pl.BlockSpec((lat_len, HEAD_DIM), lambda b, h, i: (b, vb + h)),
        ]
        args += [p, p]
    m_rows = group * tq
    return pl.pallas_call(
        functools.partial(_flash_kernel, group=group, tk=tk, n_chunks=n_chunks, has_sink=has_sink),
        grid=(n_batch, kv_heads, nq),
        in_specs=in_specs,
        out_specs=pl.BlockSpec((tq, gw), lambda b, h, i: (b * nq + i, h)),
        out_shape=jax.ShapeDtypeStruct((n_batch * q_rows, kv_heads * gw), BF16),
        scratch_shapes=[pltpu.VMEM((m_rows, 1), F32), pltpu.VMEM((m_rows, 1), F32),
                        pltpu.VMEM((m_rows, HEAD_DIM), F32)],
        compiler_params=_params(("arbitrary", "arbitrary", "arbitrary")),
    )(*args)


def _window_kernel(sink_ref, q_ref, k_ref, v_ref, kc_ref, vc_ref, o_ref, *, seq_len):
    kvh = pl.program_id(1)
    iq = pl.program_id(2)
    tq = q_ref.shape[0]
    win = tq + 2 * A_WINDOW
    q0 = iq * tq
    start = pl.multiple_of(jnp.clip(q0 - A_WINDOW, 0, seq_len - win), A_WINDOW)
    q = jnp.concatenate([q_ref[:, hh * HEAD_DIM:(hh + 1) * HEAD_DIM] for hh in range(A_GROUP)], axis=0)
    m_rows = A_GROUP * tq

    s = _dot_nt(q, k_ref[pl.ds(start, win), :])
    qpos = q0 + (lax.broadcasted_iota(jnp.int32, (m_rows, win), 0) & (tq - 1))
    kpos = start + lax.broadcasted_iota(jnp.int32, (m_rows, win), 1)
    s = jnp.where(jnp.abs(kpos - qpos) <= A_WINDOW, s, NEG_INF)
    sc = _dot_nt(q, kc_ref[...])
    sk = jnp.concatenate(
        [jnp.full((tq, 1), sink_ref[kvh * A_GROUP + hh], F32) for hh in range(A_GROUP)], axis=0)
    m = jnp.maximum(jnp.maximum(jnp.max(s, axis=-1, keepdims=True), jnp.max(sc, axis=-1, keepdims=True)), sk)
    p = jnp.exp(s - m)
    pc = jnp.exp(sc - m)
    l = jnp.sum(p, axis=-1, keepdims=True) + jnp.sum(pc, axis=-1, keepdims=True) + jnp.exp(sk - m)
    o = (_dot(p.astype(BF16), v_ref[pl.ds(start, win), :]) + _dot(pc.astype(BF16), vc_ref[...])) / l
    for hh in range(A_GROUP):
        o_ref[:, hh * HEAD_DIM:(hh + 1) * HEAD_DIM] = o[hh * tq:(hh + 1) * tq].astype(o_ref.dtype)


def _window_attention(p, sink, *, n_batch, seq_len, ctx_len, k_col0, v_col0, tq=128):
    nq = seq_len // tq
    gw = A_GROUP * HEAD_DIM
    kb, vb = k_col0 // HEAD_DIM, v_col0 // HEAD_DIM
    c0b = n_batch * seq_len // ctx_len
    return pl.pallas_call(
        functools.partial(_window_kernel, seq_len=seq_len),
        grid=(n_batch, A_KV_HEADS, nq),
        in_specs=[
            pl.BlockSpec(memory_space=pltpu.SMEM),
            pl.BlockSpec((tq, gw), lambda b, h, i: (b * nq + i, h)),
            pl.BlockSpec((seq_len, HEAD_DIM), lambda b, h, i: (b, kb + h)),
            pl.BlockSpec((seq_len, HEAD_DIM), lambda b, h, i: (b, vb + h)),
            pl.BlockSpec((ctx_len, HEAD_DIM), lambda b, h, i: (c0b + b, kb + h)),
            pl.BlockSpec((ctx_len, HEAD_DIM), lambda b, h, i: (c0b + b, vb + h)),
        ],
        out_specs=pl.BlockSpec((tq, gw), lambda b, h, i: (b * nq + i, h)),
        out_shape=jax.ShapeDtypeStruct((n_batch * seq_len, A_KV_HEADS * gw), BF16),
        compiler_params=_params(("arbitrary", "arbitrary", "arbitrary")),
    )(sink, p, p, p, p, p)


def _neighborhood_kernel(q_ref, k_ref, v_ref, kc_ref, vc_ref, bias_ref, o_ref, *, n_rows):
    n_keys = B_WIN_ROWS * GRID_W

    def body(r, carry):
        rstart = jnp.clip(r - B_WIN_ROWS // 2, 0, n_rows - B_WIN_ROWS)
        d = r - rstart
        q0 = pl.multiple_of(r * GRID_W, GRID_W)
        k0 = pl.multiple_of(rstart * GRID_W, GRID_W)
        q = q_ref[pl.ds(q0, GRID_W), :]
        s = _dot_nt(q, k_ref[pl.ds(k0, n_keys), :]) + bias_ref[0, d]
        sc = _dot_nt(q, kc_ref[...])
        m = jnp.maximum(jnp.max(s, axis=-1, keepdims=True), jnp.max(sc, axis=-1, keepdims=True))
        p = jnp.exp(s - m)
        pc = jnp.exp(sc - m)
        l = jnp.sum(p, axis=-1, keepdims=True) + jnp.sum(pc, axis=-1, keepdims=True)
        o = (_dot(p.astype(BF16), v_ref[pl.ds(k0, n_keys), :]) + _dot(pc.astype(BF16), vc_ref[...])) / l
        o_ref[pl.ds(q0, GRID_W), :] = o.astype(o_ref.dtype)
        return carry

    lax.fori_loop(0, n_rows, body, 0)


def _neighborhood_bias(rpb):
    qc = np.arange(GRID_W)[:, None]
    kc = np.arange(GRID_W)[None, :]
    wstart = np.clip(qc - B_WIN_COLS // 2, 0, GRID_W - B_WIN_COLS)
    col_ok = (kc >= wstart) & (kc < wstart + B_WIN_COLS)
    dcol = np.clip(kc - qc + B_WIN_COLS - 1, 0, 2 * B_WIN_COLS - 2)
    d = np.arange(B_WIN_ROWS)[:, None]
    kk = np.arange(B_WIN_ROWS)[None, :]
    drow = kk - d + B_WIN_ROWS - 1
    bias = rpb.astype(F32)[:, drow][:, :, :, dcol]
    bias = jnp.where(col_ok[None, None, None], bias, NEG_INF)
    bias = jnp.transpose(bias, (0, 1, 3, 2, 4))
    return bias.reshape(rpb.shape[0], B_WIN_ROWS, GRID_W, B_WIN_ROWS * GRID_W)


def _neighborhood_attention(p, bias, *, n_batch, seq_len, ctx_len, q_col0, k_col0, v_col0):
    qb, kb, vb = q_col0 // HEAD_DIM, k_col0 // HEAD_DIM, v_col0 // HEAD_DIM
    c0b = n_batch * seq_len // ctx_len
    n_rows = seq_len // GRID_W
    return pl.pallas_call(
        functools.partial(_neighborhood_kernel, n_rows=n_rows),
        grid=(n_batch, B_HEADS),
        in_specs=[
            pl.BlockSpec((seq_len, HEAD_DIM), lambda b, h: (b, qb + h)),
            pl.BlockSpec((seq_len, HEAD_DIM), lambda b, h: (b, kb + h)),
            pl.BlockSpec((seq_len, HEAD_DIM), lambda b, h: (b, vb + h)),
            pl.BlockSpec((ctx_len, HEAD_DIM), lambda b, h: (c0b + b, kb + h)),
            pl.BlockSpec((ctx_len, HEAD_DIM), lambda b, h: (c0b + b, vb + h)),
            pl.BlockSpec((1, B_WIN_ROWS, GRID_W, B_WIN_ROWS * GRID_W), lambda b, h: (h, 0, 0, 0)),
        ],
        out_specs=pl.BlockSpec((seq_len, HEAD_DIM), lambda b, h: (b, h)),
        out_shape=jax.ShapeDtypeStruct((n_batch * seq_len, B_HEADS * HEAD_DIM), BF16),
        compiler_params=_params(("arbitrary", "arbitrary")),
    )(p, p, p, p, p, bias)


def _residual_ln_kernel(*refs, n_mm):
    a_refs = refs[:n_mm]
    w_refs = refs[n_mm:2 * n_mm]
    if n_mm:
        x_ref, g_ref, lg_ref, lb_ref, o_ref = refs[2 * n_mm:]
        y = _dot(a_refs[0][...], w_refs[0][...])
        for a_ref, w_ref in zip(a_refs[1:], w_refs[1:]):
            y = y + _dot(a_ref[...], w_ref[...])
    else:
        y_ref, x_ref, g_ref, lg_ref, lb_ref, o_ref = refs
        y = y_ref[...]
    z = DEEPNORM_ALPHA * x_ref[...] + g_ref[0] * y
    mu = jnp.mean(z, axis=-1, keepdims=True)
    zc = z - mu
    var = jnp.mean(zc * zc, axis=-1, keepdims=True)
    o_ref[...] = (zc * lax.rsqrt(var + 1e-5)) * lg_ref[...] + lb_ref[...]


def _residual_ln(x, gate, ln_g, ln_b, *, seq_len, n_lat_rows, mm=(), y=None, n_out_rows=None, tm=256):
    t, d = x.shape
    n_out_rows = n_out_rows or t
    n_batch = n_lat_rows // seq_len

    def mod_map(i):
        return (jnp.minimum((i * tm) // seq_len, n_batch), 0, 0)

    in_specs, args = [], []
    for a, _ in mm:
        in_specs.append(pl.BlockSpec((tm, a.shape[1]), lambda i: (i, 0)))
        args.append(a)
    for _, w in mm:
        in_specs.append(pl.BlockSpec(w.shape, lambda i: (0, 0)))
        args.append(w)
    if not mm:
        in_specs.append(pl.BlockSpec((tm, d), lambda i: (i, 0)))
        args.append(y)
    in_specs += [
        pl.BlockSpec((tm, d), lambda i: (i, 0)),
        pl.BlockSpec((1, 1, d), mod_map),
        pl.BlockSpec((1, d), lambda i: (0, 0)),
        pl.BlockSpec((1, d), lambda i: (0, 0)),
    ]
    args += [x, gate, ln_g.reshape(1, d), ln_b.reshape(1, d)]
    return pl.pallas_call(
        functools.partial(_residual_ln_kernel, n_mm=len(mm)),
        grid=(n_out_rows // tm,),
        in_specs=in_specs,
        out_specs=pl.BlockSpec((tm, d), lambda i: (i, 0)),
        out_shape=jax.ShapeDtypeStruct((n_out_rows, d), F32),
        compiler_params=_params(("arbitrary",)),
    )(*args)


def _top_values(x, out_scr):
    rows = x.shape[0]
    ridx = lax.broadcasted_iota(jnp.int32, x.shape, 0).astype(F32)
    for k in range(PEER_TOPK):
        m = jnp.max(x, axis=0, keepdims=True)
        out_scr[k:k + 1, :] = m
        first = jnp.min(jnp.where(x == m, ridx, float(rows)), axis=0, keepdims=True)
        x = jnp.where(ridx == first, -jnp.inf, x)


def _select_kernel(q_ref, sk_ref, s1_ref, e1_ref, s2_ref, e2_ref, thr_ref, t1_scr, t2_scr, cand_scr, best_scr):
    q = q_ref[...].astype(BF16)
    s1 = _dot_nt(sk_ref[0, 0], q[:, :HEAD_DIM])
    s2 = _dot_nt(sk_ref[0, 1], q[:, HEAD_DIM:])
    _top_values(s1, t1_scr)
    _top_values(s2, t2_scr)
    cand_scr[0:16, :] = t1_scr[0:1, :] + t2_scr[0:16, :]
    sub = lax.broadcasted_iota(jnp.int32, (8, 1), 0)
    for a in range(1, 8):
        c = t1_scr[a:a + 1, :] + t2_scr[0:8, :]
        cand_scr[8 + 8 * a:16 + 8 * a, :] = jnp.where(sub < PEER_TOPK // (a + 1), c, -jnp.inf)
    cand_scr[72:80, :] = t1_scr[8:16, :] + t2_scr[0:1, :]
    _top_values(cand_scr[...], best_scr)
    best = best_scr[...]
    z = jnp.sum(jnp.exp(best - best[0:1, :]), axis=0, keepdims=True)
    s1_ref[0] = s1
    s2_ref[0] = s2
    e1_ref[0] = jnp.exp(s1 - t1_scr[0:1, :]) / z
    e2_ref[0] = jnp.exp(s2 - t2_scr[0:1, :])
    thr_ref[0] = best[PEER_TOPK - 1:PEER_TOPK, :]


def _peer_select(q, subkeys, tt=256):
    t = q.shape[0]
    nk = PEER_N_KEYS
    big = jax.ShapeDtypeStruct((PEER_HEADS, nk, t), F32)
    big_spec = pl.BlockSpec((1, nk, tt), lambda i, h: (h, 0, i))
    return pl.pallas_call(
        _select_kernel,
        grid=(t // tt, PEER_HEADS),
        in_specs=[
            pl.BlockSpec((tt, 2 * HEAD_DIM), lambda i, h: (i, h)),
            pl.BlockSpec((1, 2, nk, HEAD_DIM), lambda i, h: (h, 0, 0, 0)),
        ],
        out_specs=[big_spec, big_spec, big_spec, big_spec, pl.BlockSpec((1, 1, tt), lambda i, h: (h, 0, i))],
        out_shape=[big, big, big, big, jax.ShapeDtypeStruct((PEER_HEADS, 1, t), F32)],
        scratch_shapes=[pltpu.VMEM((PEER_TOPK, tt), F32), pltpu.VMEM((PEER_TOPK, tt), F32),
                        pltpu.VMEM((80, tt), F32), pltpu.VMEM((PEER_TOPK, tt), F32)],
        compiler_params=_params(("arbitrary", "arbitrary")),
    )(q, subkeys)


def _peer_kernel(h_ref, u_ref, vt_ref, s1_ref, e1_ref, s2_ref, e2_ref, thr_ref, o_ref, a_scr, w_scr, acc_scr):
    e = pl.program_id(1)
    te, tt = a_scr.shape
    nk = PEER_N_KEYS

    @pl.when(e == 0)
    def _():
        acc_scr[...] = jnp.zeros_like(acc_scr)

    a_scr[...] = _dot_nt(u_ref[...], h_ref[...])
    for tg in range(tt // 128):
        lanes = slice(tg * 128, (tg + 1) * 128)

        def body(il, carry):
            r0 = pl.multiple_of(il * nk, nk)
            a = a_scr[pl.ds(r0, nk), lanes]
            act = 0.5 * a * (1.0 + lax.erf(a * (2.0 ** -0.5)))
            gate = jnp.zeros((nk, 128), F32)
            for hd in range(PEER_HEADS):
                cand = s1_ref[hd, il, :, lanes] + s2_ref[hd, :, lanes]
                g = e1_ref[hd, il, :, lanes] * e2_ref[hd, :, lanes]
                gate = gate + jnp.where(cand >= thr_ref[hd, :, lanes], g, 0.0)
            w_scr[pl.ds(r0, nk), lanes] = (gate * act).astype(BF16)
            return carry

        lax.fori_loop(0, te // nk, body, 0)

    acc_scr[...] += _dot(vt_ref[...], w_scr[...])

    @pl.when(e == pl.num_programs(1) - 1)
    def _():
        o_ref[...] = acc_scr[...].T


def _peer_experts(h, u, vt, s1, e1, s2, e2, thr, *, tt=512, te=1024):
    t, d = h.shape
    n_exp = u.shape[0]
    nk = PEER_N_KEYS
    ni = te // nk
    row_spec = pl.BlockSpec((PEER_HEADS, ni, 1, tt), lambda i, e: (0, e, 0, i))
    col_spec = pl.BlockSpec((PEER_HEADS, nk, tt), lambda i, e: (0, 0, i))
    return pl.pallas_call(
        _peer_kernel,
        grid=(t // tt, n_exp // te),
        in_specs=[
            pl.BlockSpec((tt, d), lambda i, e: (i, 0)),
            pl.BlockSpec((te, d), lambda i, e: (e, 0)),
            pl.BlockSpec((d, te), lambda i, e: (0, e)),
            row_spec, row_spec, col_spec, col_spec,
            pl.BlockSpec((PEER_HEADS, 1, tt), lambda i, e: (0, 0, i)),
        ],
        out_specs=pl.BlockSpec((tt, d), lambda i, e: (i, 0)),
        out_shape=jax.ShapeDtypeStruct((t, d), F32),
        scratch_shapes=[pltpu.VMEM((te, tt), F32), pltpu.VMEM((te, tt), BF16), pltpu.VMEM((d, tt), F32)],
        compiler_params=_params(("arbitrary", "arbitrary")),
    )(h, u, vt, s1.reshape(PEER_HEADS, nk, 1, t), e1.reshape(PEER_HEADS, nk, 1, t), s2, e2, thr)


def _rope_tables(seq_len):
    t = jnp.arange(seq_len, dtype=jnp.int32)
    row = (t // GRID_W).astype(F32)
    col = (t % GRID_W).astype(F32)
    n_freq = HEAD_DIM // 4
    inv_freq = ROPE_THETA ** (-jnp.arange(n_freq, dtype=F32) / n_freq)
    ang_r = row[:, None] * inv_freq[None, :]
    ang_c = col[:, None] * inv_freq[None, :]
    ang = jnp.concatenate([ang_r, ang_r, ang_c, ang_c], axis=-1)
    sign = np.tile(np.repeat(np.array([-1.0, 1.0], np.float32), n_freq), 2)
    return jnp.cos(ang), jnp.sin(ang) * sign


def _col_flags(sizes_and_flags):
    out = []
    for size, fl in sizes_and_flags:
        out += [fl] * (size // FLAG_GROUP)
    return jnp.asarray(out, jnp.int32)


def kernel(x, c, ctx, c_ctx, mod_w, mod_b, ln_g, ln_b, ab_w_in, ab_w_out, a_sink, b_rpb, c_w_in, c_w_out,
           c_q_gain, c_k_gain, peer_wq, peer_subkeys, peer_u, peer_v):
    n_batch, seq_len, d = x.shape
    ctx_len = ctx.shape[1]
    n_layers = mod_w.shape[0]
    n_lat = n_batch * seq_len
    n_ctx = n_batch * ctx_len
    assert seq_len % ROW_TILE == 0 and n_ctx % ROW_TILE == 0 and seq_len % GRID_W == 0
    geo = dict(seq_len=seq_len, n_lat_rows=n_lat)

    xs = jnp.concatenate([x.reshape(n_lat, d), ctx.reshape(n_ctx, d)], axis=0)
    cos, sin_signed = _rope_tables(seq_len)

    c_rows = jnp.concatenate([c, c_ctx[None, :], jnp.zeros((8 - n_batch - 1, d), F32)], axis=0)
    mod = _modulation(c_rows, mod_w, mod_b).reshape(n_layers, 8, 6, d)

    def mod_vec(layer, k):
        return mod[layer, :n_batch + 1, k, :].reshape(n_batch + 1, 1, d)

    a_q, a_kv, b_w = A_Q_HEADS * HEAD_DIM, A_KV_HEADS * HEAD_DIM, B_HEADS * HEAD_DIM
    ab_flags = _col_flags([(a_q, FLAG_ROPE | FLAG_SCALE), (a_kv, FLAG_ROPE), (a_kv, 0),
                           (b_w, FLAG_SCALE), (b_w, 0), (b_w, 0)])
    c_q, c_kv = C_Q_HEADS * HEAD_DIM, C_KV_HEADS * HEAD_DIM
    c_flags = _col_flags([(c_q, FLAG_RMS_Q | FLAG_ROPE | FLAG_SCALE), (c_kv, FLAG_RMS_K | FLAG_ROPE), (c_kv, 0)])
    plain_flags = _col_flags([(peer_wq.shape[2], 0)])
    no_gain = jnp.zeros((8, HEAD_DIM), F32)

    for layer in range(n_layers):
        last = layer == n_layers - 1
        i = layer // 2
        sh1, sc1, g1, sh2, sc2, g2 = (mod_vec(layer, k) for k in range(6))

        if layer % 2 == 0:
            p = _projection(xs, sc1, sh1, ab_w_in[i].astype(BF16), ab_flags, cos, sin_signed, no_gain,
                            out_dtype=BF16, **geo)
            ka0, va0 = a_q, a_q + a_kv
            qb0 = a_q + 2 * a_kv
            kb0, vb0 = qb0 + b_w, qb0 + 2 * b_w
            out_a = _window_attention(p, a_sink[i], n_batch=n_batch, seq_len=seq_len, ctx_len=ctx_len,
                                      k_col0=ka0, v_col0=va0)
            out_b = _neighborhood_attention(p, _neighborhood_bias(b_rpb[i]), n_batch=n_batch, seq_len=seq_len,
                                            ctx_len=ctx_len, q_col0=qb0, k_col0=kb0, v_col0=vb0)
            ctx_a = _flash(p, a_sink[i], n_batch=n_batch, q_rows=ctx_len, q_row0=n_lat, q_col0=0,
                           kv_heads=A_KV_HEADS, group=A_GROUP, k_col0=ka0, v_col0=va0,
                           ctx_row0=n_lat, ctx_len=ctx_len, lat_len=0, tq=ctx_len)
            ctx_b = _flash(p, None, n_batch=n_batch, q_rows=ctx_len, q_row0=n_lat, q_col0=qb0,
                           kv_heads=B_HEADS, group=1, k_col0=kb0, v_col0=vb0,
                           ctx_row0=n_lat, ctx_len=ctx_len, lat_len=0, tq=ctx_len)
            w_out = ab_w_out[i].astype(BF16)
            mm = [(jnp.concatenate([out_a, ctx_a], axis=0), w_out[:a_q]),
                  (jnp.concatenate([out_b, ctx_b], axis=0), w_out[a_q:])]
        else:
            gains = jnp.concatenate([c_q_gain[i][None], c_k_gain[i][None], jnp.zeros((6, HEAD_DIM), F32)], axis=0)
            p = _projection(xs, sc1, sh1, c_w_in[i].astype(BF16), c_flags, cos, sin_signed, gains,
                            out_dtype=BF16, **geo)
            kc0, vc0 = c_q, c_q + c_kv
            out_c = _flash(p, None, n_batch=n_batch, q_rows=seq_len, q_row0=0, q_col0=0,
                           kv_heads=C_KV_HEADS, group=C_GROUP, k_col0=kc0, v_col0=vc0,
                           ctx_row0=n_lat, ctx_len=ctx_len, lat_len=seq_len, tq=128)
            ctx_c = _flash(p, None, n_batch=n_batch, q_rows=ctx_len, q_row0=n_lat, q_col0=0,
                           kv_heads=C_KV_HEADS, group=C_GROUP, k_col0=kc0, v_col0=vc0,
                           ctx_row0=n_lat, ctx_len=ctx_len, lat_len=0, tq=ctx_len)
            mm = [(jnp.concatenate([out_c, ctx_c], axis=0), c_w_out[i].astype(BF16))]

        xs = _residual_ln(xs, g1, ln_g[layer, 0], ln_b[layer, 0], mm=mm, **geo)

        q, h = _projection(xs, sc2, sh2, peer_wq[layer].astype(BF16), plain_flags, cos, sin_signed, no_gain,
                           out_dtype=F32, emit_h=True, **geo)
        s1, e1, s2, e2, thr = _peer_select(q, peer_subkeys[layer].astype(BF16))
        y = _peer_experts(h, peer_u[layer].astype(BF16), peer_v[layer].astype(BF16).T, s1, e1, s2, e2, thr)
        xs = _residual_ln(xs, g2, ln_g[layer, 1], ln_b[layer, 1], y=y,
                          n_out_rows=n_lat if last else None, **geo)

    return xs.reshape(n_batch, seq_len, d)
```

```python
import functools

import numpy as np
import jax
import jax.numpy as jnp
from jax import lax
from jax.experimental import pallas as pl
from jax.experimental.pallas import tpu as pltpu

F32 = jnp.float32
BF16 = jnp.bfloat16

DEPTH = 4
GRID_W = 64
HEAD_DIM = 128
ATTN_SCALE = HEAD_DIM ** -0.5
LOG2E = 1.4426950408889634
Q_SCALE = ATTN_SCALE * LOG2E
ROPE_THETA = 10000.0
NEG_INF = -1e30

A_Q_HEADS = 8
A_KV_HEADS = 2
A_GROUP = A_Q_HEADS // A_KV_HEADS
A_WINDOW = 128
B_HEADS = 8
B_WIN_ROWS = 8
B_WIN_COLS = 16
C_Q_HEADS = 16
C_KV_HEADS = 4
C_GROUP = C_Q_HEADS // C_KV_HEADS

PEER_HEADS = 8
PEER_N_KEYS = 128
PEER_TOPK = 16
GATE_ROWS = 32

DEEPNORM_ALPHA = (2 * DEPTH) ** 0.25

VMEM_LIMIT_BYTES = 56 * 1024 * 1024

ROW_TILE = 512
COL_TILE = 512
FLAG_GROUP = 256


def _dot(a, b):
    return jnp.dot(a, b, preferred_element_type=F32)


def _dot_nt(a, b):
    return lax.dot_general(a, b, (((1,), (1,)), ((), ())), preferred_element_type=F32)


def _params(sem, vmem=None):
    return pltpu.CompilerParams(dimension_semantics=sem, vmem_limit_bytes=vmem or VMEM_LIMIT_BYTES)


def _mod_kernel(c_ref, w_ref, b_ref, o_ref):
    c = c_ref[...]
    a = c * (1.0 / (1.0 + jnp.exp(-c)))
    o_ref[0] = _dot(a.astype(BF16), w_ref[0].astype(BF16)) + b_ref[0]


def _modulation(c_rows, mod_w, mod_b):
    n_layers, d, n = mod_w.shape
    tn = 1024
    rows = c_rows.shape[0]
    return pl.pallas_call(
        _mod_kernel,
        grid=(n_layers, n // tn),
        in_specs=[
            pl.BlockSpec((rows, d), lambda l, j: (0, 0)),
            pl.BlockSpec((1, d, tn), lambda l, j: (l, 0, j)),
            pl.BlockSpec((1, 1, tn), lambda l, j: (l, 0, j)),
        ],
        out_specs=pl.BlockSpec((1, rows, tn), lambda l, j: (l, 0, j)),
        out_shape=jax.ShapeDtypeStruct((n_layers, rows, n), F32),
        compiler_params=_params(("arbitrary", "arbitrary")),
    )(c_rows, mod_w, mod_b.reshape(n_layers, 1, n))


FLAG_RMS_Q = 1
FLAG_RMS_K = 2
FLAG_ROPE = 4
FLAG_SCALE = 8


def _proj_kernel(flags_ref, x_ref, sc_ref, sh_ref, w_ref, cos_ref, sin_ref, gain_ref, o_ref, *rest,
                 n_lat_tiles, emit_h):
    if emit_h:
        h_out_ref, h_scr, acc_scr = rest
    else:
        h_scr, acc_scr = rest
    i = pl.program_id(0)
    j = pl.program_id(1)
    tn = o_ref.shape[1]

    @pl.when(j == 0)
    def _():
        h = (x_ref[...] * (1.0 + sc_ref[0]) + sh_ref[0]).astype(BF16)
        h_scr[...] = h
        if emit_h:
            h_out_ref[...] = h

    acc_scr[...] = _dot(h_scr[...], w_ref[...])
    is_lat = i < n_lat_tiles
    lane = lax.broadcasted_iota(jnp.int32, (1, HEAD_DIM), 1)
    odd_seg = ((lane // (HEAD_DIM // 4)) % 2) == 1

    for g in range(tn // FLAG_GROUP):
        fl = flags_ref[j * (tn // FLAG_GROUP) + g]
        rms = fl & 3
        do_rope = jnp.logical_and((fl & FLAG_ROPE) != 0, is_lat)
        scale = jnp.where((fl & FLAG_SCALE) != 0, Q_SCALE, 1.0).astype(F32)
        for hh in range(FLAG_GROUP // HEAD_DIM):
            c0 = g * FLAG_GROUP + hh * HEAD_DIM
            cols = slice(c0, c0 + HEAD_DIM)

            @pl.when(rms != 0)
            def _():
                y = acc_scr[:, cols]
                gain = gain_ref[pl.ds(rms - 1, 1), :]
                y = y * lax.rsqrt(jnp.mean(y * y, axis=-1, keepdims=True) + 1e-6)
                acc_scr[:, cols] = y * gain

            @pl.when(do_rope)
            def _():
                y = acc_scr[:, cols]
                r_dn = pltpu.roll(y, HEAD_DIM // 4, 1)
                r_up = pltpu.roll(y, 3 * HEAD_DIM // 4, 1)
                y = y * cos_ref[...] + jnp.where(odd_seg, r_dn, r_up) * sin_ref[...]
                o_ref[:, cols] = (y * scale).astype(o_ref.dtype)

            @pl.when(jnp.logical_not(do_rope))
            def _():
                o_ref[:, cols] = (acc_scr[:, cols] * scale).astype(o_ref.dtype)


def _projection(x, sc, sh, w, flags, cos, sin_signed, gains, *, seq_len, n_lat_rows, out_dtype, emit_h=False):
    t, d = x.shape
    n = w.shape[1]
    tm, tn = ROW_TILE, COL_TILE
    n_lat_tiles = n_lat_rows // tm
    pos_tiles = seq_len // tm
    n_batch = n_lat_rows // seq_len

    def mod_map(i, j, fl):
        return (jnp.minimum((i * tm) // seq_len, n_batch), 0, 0)

    out_shape = [jax.ShapeDtypeStruct((t, n), out_dtype)]
    out_specs = [pl.BlockSpec((tm, tn), lambda i, j, fl: (i, j))]
    if emit_h:
        out_shape.append(jax.ShapeDtypeStruct((t, d), BF16))
        out_specs.append(pl.BlockSpec((tm, d), lambda i, j, fl: (i, 0)))
    grid_spec = pltpu.PrefetchScalarGridSpec(
        num_scalar_prefetch=1,
        grid=(t // tm, n // tn),
        in_specs=[
            pl.BlockSpec((tm, d), lambda i, j, fl: (i, 0)),
            pl.BlockSpec((1, 1, d), mod_map),
            pl.BlockSpec((1, 1, d), mod_map),
            pl.BlockSpec((d, tn), lambda i, j, fl: (0, j)),
            pl.BlockSpec((tm, HEAD_DIM), lambda i, j, fl: (i % pos_tiles, 0)),
            pl.BlockSpec((tm, HEAD_DIM), lambda i, j, fl: (i % pos_tiles, 0)),
            pl.BlockSpec((8, HEAD_DIM), lambda i, j, fl: (0, 0)),
        ],
        out_specs=out_specs,
        scratch_shapes=[pltpu.VMEM((tm, d), BF16), pltpu.VMEM((tm, tn), F32)],
    )
    res = pl.pallas_call(
        functools.partial(_proj_kernel, n_lat_tiles=n_lat_tiles, emit_h=emit_h),
        grid_spec=grid_spec,
        out_shape=out_shape,
        compiler_params=_params(("arbitrary", "arbitrary")),
    )(flags, x, sc, sh, w, cos, sin_signed, gains)
    return res if emit_h else res[0]


def _flash_kernel(sink_ref, q_ref, kc_ref, vc_ref, *rest, group, tk, n_chunks, has_sink, sub, aliased):
    rest = list(rest)
    if n_chunks:
        kl_ref, vl_ref = rest[:2]
        rest = rest[2:]
    if aliased:
        rest = rest[1:]
    o_ref, m_scr, l_scr, acc_scr = rest
    tq = q_ref.shape[0]
    m_rows = group * tq
    q = jnp.concatenate([q_ref[:, hh * HEAD_DIM:(hh + 1) * HEAD_DIM] for hh in range(group)], axis=0)

    def update(k, v, first):
        s = _dot_nt(q, k)
        n = s.shape[1]
        for r in range(m_rows // sub):
            rows = slice(r * sub, (r + 1) * sub)
            sr = s[rows]
            mx = jnp.max(sr, axis=-1, keepdims=True)
            if first:
                m_new = mx
            else:
                m_old = m_scr[rows]
                m_new = jnp.maximum(m_old, mx)
                a = jnp.exp2(m_old - m_new)
            p = jnp.exp2(sr - m_new)
            psum = p[:, :HEAD_DIM]
            for c in range(1, n // HEAD_DIM):
                psum = psum + p[:, c * HEAD_DIM:(c + 1) * HEAD_DIM]
            pv = _dot(p.astype(BF16), v)
            if first:
                l_scr[rows] = psum
                acc_scr[rows] = pv
            else:
                l_scr[rows] = a * l_scr[rows] + psum
                acc_scr[rows] = a * acc_scr[rows] + pv
            m_scr[rows] = m_new

    update(kc_ref[...], vc_ref[...], True)
    if n_chunks:
        def body(c, carry):
            r0 = pl.multiple_of(c * tk, tk)
            update(kl_ref[pl.ds(r0, tk), :], vl_ref[pl.ds(r0, tk), :], False)
            return carry

        lax.fori_loop(0, n_chunks, body, 0)

    m = m_scr[...]
    l = jnp.sum(l_scr[...], axis=-1, keepdims=True)
    acc = acc_scr[...]
    if has_sink:
        kvh = pl.program_id(1)
        sk = jnp.concatenate(
            [jnp.full((tq, 1), sink_ref[kvh * group + hh], F32) for hh in range(group)], axis=0)
        m_new = jnp.maximum(m, sk)
        a = jnp.exp2(m - m_new)
        l = a * l + jnp.exp2(sk - m_new)
        acc = a * acc
    o = acc / l
    for hh in range(group):
        o_ref[:, hh * HEAD_DIM:(hh + 1) * HEAD_DIM] = o[hh * tq:(hh + 1) * tq].astype(o_ref.dtype)


def _flash(p, sink, *, n_batch, q_rows, q_row0, q_col0, kv_heads, group, k_col0, v_col0,
           ctx_row0, ctx_len, lat_len, tq, total_rows, out=None, tk=512, sub=256):
    nq = q_rows // tq
    n_chunks = lat_len // tk if lat_len else 0
    has_sink = sink is not None
    if sink is None:
        sink = jnp.zeros((kv_heads * group,), F32)
    gw = group * HEAD_DIM
    kb, vb, qb = k_col0 // HEAD_DIM, v_col0 // HEAD_DIM, q_col0 // gw
    q0b, c0b = q_row0 // tq, ctx_row0 // ctx_len
    in_specs = [
        pl.BlockSpec(memory_space=pltpu.SMEM),
        pl.BlockSpec((tq, gw), lambda b, h, i: (q0b + b * nq + i, qb + h)),
        pl.BlockSpec((ctx_len, HEAD_DIM), lambda b, h, i: (c0b + b, kb + h)),
        pl.BlockSpec((ctx_len, HEAD_DIM), lambda b, h, i: (c0b + b, vb + h)),
    ]
    args = [sink, p, p, p]
    if n_chunks:
        in_specs += [
            pl.BlockSpec((lat_len, HEAD_DIM), lambda b, h, i: (b, kb + h)),
            pl.BlockSpec((lat_len, HEAD_DIM), lambda b, h, i: (b, vb + h)),
        ]
        args += [p, p]
    aliases = {}
    if out is not None:
        aliases = {len(args): 0}
        in_specs.append(pl.BlockSpec(memory_space=pl.ANY))
        args.append(out)
    m_rows = group * tq
    return pl.pallas_call(
        functools.partial(_flash_kernel, group=group, tk=tk, n_chunks=n_chunks, has_sink=has_sink,
                          sub=min(sub, m_rows), aliased=out is not None),
        grid=(n_batch, kv_heads, nq),
        in_specs=in_specs,
        out_specs=pl.BlockSpec((tq, gw), lambda b, h, i: (q0b + b * nq + i, h)),
        out_shape=jax.ShapeDtypeStruct((total_rows, kv_heads * gw), BF16),
        input_output_aliases=aliases,
        scratch_shapes=[pltpu.VMEM((m_rows, 1), F32), pltpu.VMEM((m_rows, HEAD_DIM), F32),
                        pltpu.VMEM((m_rows, HEAD_DIM), F32)],
        compiler_params=_params(("arbitrary", "arbitrary", "arbitrary")),
    )(*args)


def _window_kernel(sink_ref, q_ref, k_ref, v_ref, kc_ref, vc_ref, o_ref, *, seq_len):
    kvh = pl.program_id(1)
    iq = pl.program_id(2)
    tq = q_ref.shape[0]
    win = tq + 2 * A_WINDOW
    q0 = iq * tq
    start = pl.multiple_of(jnp.clip(q0 - A_WINDOW, 0, seq_len - win), A_WINDOW)
    q = jnp.concatenate([q_ref[:, hh * HEAD_DIM:(hh + 1) * HEAD_DIM] for hh in range(A_GROUP)], axis=0)
    m_rows = A_GROUP * tq

    s = _dot_nt(q, k_ref[pl.ds(start, win), :])
    qpos = q0 + (lax.broadcasted_iota(jnp.int32, (m_rows, win), 0) & (tq - 1))
    kpos = start + lax.broadcasted_iota(jnp.int32, (m_rows, win), 1)
    s = jnp.where(jnp.abs(kpos - qpos) <= A_WINDOW, s, NEG_INF)
    sc = _dot_nt(q, kc_ref[...])
    sk = jnp.concatenate(
        [jnp.full((tq, 1), sink_ref[kvh * A_GROUP + hh], F32) for hh in range(A_GROUP)], axis=0)
    m = jnp.maximum(jnp.maximum(jnp.max(s, axis=-1, keepdims=True), jnp.max(sc, axis=-1, keepdims=True)), sk)
    p = jnp.exp2(s - m)
    pc = jnp.exp2(sc - m)
    l = jnp.sum(p, axis=-1, keepdims=True) + jnp.sum(pc, axis=-1, keepdims=True) + jnp.exp2(sk - m)
    o = (_dot(p.astype(BF16), v_ref[pl.ds(start, win), :]) + _dot(pc.astype(BF16), vc_ref[...])) / l
    for hh in range(A_GROUP):
        o_ref[:, hh * HEAD_DIM:(hh + 1) * HEAD_DIM] = o[hh * tq:(hh + 1) * tq].astype(o_ref.dtype)


def _window_attention(p, sink, *, n_batch, seq_len, ctx_len, k_col0, v_col0, total_rows, tq=128):
    nq = seq_len // tq
    gw = A_GROUP * HEAD_DIM
    kb, vb = k_col0 // HEAD_DIM, v_col0 // HEAD_DIM
    c0b = n_batch * seq_len // ctx_len
    return pl.pallas_call(
        functools.partial(_window_kernel, seq_len=seq_len),
        grid=(n_batch, A_KV_HEADS, nq),
        in_specs=[
            pl.BlockSpec(memory_space=pltpu.SMEM),
            pl.BlockSpec((tq, gw), lambda b, h, i: (b * nq + i, h)),
            pl.BlockSpec((seq_len, HEAD_DIM), lambda b, h, i: (b, kb + h)),
            pl.BlockSpec((seq_len, HEAD_DIM), lambda b, h, i: (b, vb + h)),
            pl.BlockSpec((ctx_len, HEAD_DIM), lambda b, h, i: (c0b + b, kb + h)),
            pl.BlockSpec((ctx_len, HEAD_DIM), lambda b, h, i: (c0b + b, vb + h)),
        ],
        out_specs=pl.BlockSpec((tq, gw), lambda b, h, i: (b * nq + i, h)),
        out_shape=jax.ShapeDtypeStruct((total_rows, A_KV_HEADS * gw), BF16),
        compiler_params=_params(("arbitrary", "arbitrary", "arbitrary")),
    )(sink, p, p, p, p, p)


def _neighborhood_kernel(q_ref, k_ref, v_ref, kc_ref, vc_ref, bias_ref, o_ref, *, n_rows):
    n_keys = B_WIN_ROWS * GRID_W

    def body(r, carry):
        rstart = jnp.clip(r - B_WIN_ROWS // 2, 0, n_rows - B_WIN_ROWS)
        d = r - rstart
        q0 = pl.multiple_of(r * GRID_W, GRID_W)
        k0 = pl.multiple_of(rstart * GRID_W, GRID_W)
        q = q_ref[pl.ds(q0, GRID_W), :]
        s = _dot_nt(q, k_ref[pl.ds(k0, n_keys), :]) + bias_ref[0, d]
        sc = _dot_nt(q, kc_ref[...])
        m = jnp.maximum(jnp.max(s, axis=-1, keepdims=True), jnp.max(sc, axis=-1, keepdims=True))
        p = jnp.exp2(s - m)
        pc = jnp.exp2(sc - m)
        l = jnp.sum(p, axis=-1, keepdims=True) + jnp.sum(pc, axis=-1, keepdims=True)
        o = (_dot(p.astype(BF16), v_ref[pl.ds(k0, n_keys), :]) + _dot(pc.astype(BF16), vc_ref[...])) / l
        o_ref[pl.ds(q0, GRID_W), :] = o.astype(o_ref.dtype)
        return carry

    lax.fori_loop(0, n_rows, body, 0)


def _neighborhood_bias(rpb):
    qc = np.arange(GRID_W)[:, None]
    kc = np.arange(GRID_W)[None, :]
    wstart = np.clip(qc - B_WIN_COLS // 2, 0, GRID_W - B_WIN_COLS)
    col_ok = (kc >= wstart) & (kc < wstart + B_WIN_COLS)
    dcol = np.clip(kc - qc + B_WIN_COLS - 1, 0, 2 * B_WIN_COLS - 2)
    d = np.arange(B_WIN_ROWS)[:, None]
    kk = np.arange(B_WIN_ROWS)[None, :]
    drow = kk - d + B_WIN_ROWS - 1
    bias = (rpb.astype(F32) * LOG2E)[:, drow][:, :, :, dcol]
    bias = jnp.where(col_ok[None, None, None], bias, NEG_INF)
    bias = jnp.transpose(bias, (0, 1, 3, 2, 4))
    return bias.reshape(rpb.shape[0], B_WIN_ROWS, GRID_W, B_WIN_ROWS * GRID_W)


def _neighborhood_attention(p, bias, *, n_batch, seq_len, ctx_len, q_col0, k_col0, v_col0, total_rows):
    qb, kb, vb = q_col0 // HEAD_DIM, k_col0 // HEAD_DIM, v_col0 // HEAD_DIM
    c0b = n_batch * seq_len // ctx_len
    n_rows = seq_len // GRID_W
    return pl.pallas_call(
        functools.partial(_neighborhood_kernel, n_rows=n_rows),
        grid=(n_batch, B_HEADS),
        in_specs=[
            pl.BlockSpec((seq_len, HEAD_DIM), lambda b, h: (b, qb + h)),
            pl.BlockSpec((seq_len, HEAD_DIM), lambda b, h: (b, kb + h)),
            pl.BlockSpec((seq_len, HEAD_DIM), lambda b, h: (b, vb + h)),
            pl.BlockSpec((ctx_len, HEAD_DIM), lambda b, h: (c0b + b, kb + h)),
            pl.BlockSpec((ctx_len, HEAD_DIM), lambda b, h: (c0b + b, vb + h)),
            pl.BlockSpec((1, B_WIN_ROWS, GRID_W, B_WIN_ROWS * GRID_W), lambda b, h: (h, 0, 0, 0)),
        ],
        out_specs=pl.BlockSpec((seq_len, HEAD_DIM), lambda b, h: (b, h)),
        out_shape=jax.ShapeDtypeStruct((total_rows, B_HEADS * HEAD_DIM), BF16),
        compiler_params=_params(("arbitrary", "arbitrary")),
    )(p, p, p, p, p, bias)


def _residual_ln_kernel(*refs, n_mm):
    a_refs = refs[:n_mm]
    w_refs = refs[n_mm:2 * n_mm]
    if n_mm:
        x_ref, g_ref, lg_ref, lb_ref, o_ref = refs[2 * n_mm:]
        y = _dot(a_refs[0][...], w_refs[0][...])
        for a_ref, w_ref in zip(a_refs[1:], w_refs[1:]):
            y = y + _dot(a_ref[...], w_ref[...])
    else:
        y_ref, x_ref, g_ref, lg_ref, lb_ref, o_ref = refs
        y = y_ref[...]
    z = DEEPNORM_ALPHA * x_ref[...] + g_ref[0] * y
    mu = jnp.mean(z, axis=-1, keepdims=True)
    zc = z - mu
    var = jnp.mean(zc * zc, axis=-1, keepdims=True)
    o_ref[...] = (zc * lax.rsqrt(var + 1e-5)) * lg_ref[...] + lb_ref[...]


def _residual_ln(x, gate, ln_g, ln_b, *, seq_len, n_lat_rows, mm=(), y=None, n_out_rows=None, tm=256):
    t, d = x.shape
    n_out_rows = n_out_rows or t
    n_batch = n_lat_rows // seq_len

    def mod_map(i):
        return (jnp.minimum((i * tm) // seq_len, n_batch), 0, 0)

    in_specs, args = [], []
    for a, _ in mm:
        in_specs.append(pl.BlockSpec((tm, a.shape[1]), lambda i: (i, 0)))
        args.append(a)
    for _, w in mm:
        in_specs.append(pl.BlockSpec(w.shape, lambda i: (0, 0)))
        args.append(w)
    if not mm:
        in_specs.append(pl.BlockSpec((tm, d), lambda i: (i, 0)))
        args.append(y)
    in_specs += [
        pl.BlockSpec((tm, d), lambda i: (i, 0)),
        pl.BlockSpec((1, 1, d), mod_map),
        pl.BlockSpec((1, d), lambda i: (0, 0)),
        pl.BlockSpec((1, d), lambda i: (0, 0)),
    ]
    args += [x, gate, ln_g.reshape(1, d), ln_b.reshape(1, d)]
    return pl.pallas_call(
        functools.partial(_residual_ln_kernel, n_mm=len(mm)),
        grid=(n_out_rows // tm,),
        in_specs=in_specs,
        out_specs=pl.BlockSpec((tm, d), lambda i: (i, 0)),
        out_shape=jax.ShapeDtypeStruct((n_out_rows, d), F32),
        compiler_params=_params(("arbitrary",)),
    )(*args)


def _top_values(x, out_scr):
    rows = x.shape[0]
    ridx = lax.broadcasted_iota(jnp.int32, x.shape, 0).astype(F32)
    for k in range(PEER_TOPK):
        m = jnp.max(x, axis=0, keepdims=True)
        out_scr[k:k + 1, :] = m
        first = jnp.min(jnp.where(x == m, ridx, float(rows)), axis=0, keepdims=True)
        x = jnp.where(ridx == first, -jnp.inf, x)


def _select_kernel(q_ref, sk_ref, s1_ref, e1_ref, s2_ref, e2_ref, thr_ref, t1_scr, t2_scr, cand_scr, best_scr):
    q = q_ref[...].astype(BF16)
    s1 = _dot_nt(sk_ref[0, 0], q[:, :HEAD_DIM])
    s2 = _dot_nt(sk_ref[0, 1], q[:, HEAD_DIM:])
    _top_values(s1, t1_scr)
    _top_values(s2, t2_scr)
    cand_scr[0:16, :] = t1_scr[0:1, :] + t2_scr[0:16, :]
    sub = lax.broadcasted_iota(jnp.int32, (8, 1), 0)
    for a in range(1, 8):
        c = t1_scr[a:a + 1, :] + t2_scr[0:8, :]
        cand_scr[8 + 8 * a:16 + 8 * a, :] = jnp.where(sub < PEER_TOPK // (a + 1), c, -jnp.inf)
    cand_scr[72:80, :] = t1_scr[8:16, :] + t2_scr[0:1, :]
    _top_values(cand_scr[...], best_scr)
    best = best_scr[...]
    z = jnp.sum(jnp.exp(best - best[0:1, :]), axis=0, keepdims=True)
    s1_ref[0] = s1
    s2_ref[0] = s2
    e1_ref[0] = jnp.exp(s1 - t1_scr[0:1, :]) / z
    e2_ref[0] = jnp.exp(s2 - t2_scr[0:1, :])
    thr_ref[0] = best[PEER_TOPK - 1:PEER_TOPK, :]


def _peer_select(q, subkeys, tt=256):
    t = q.shape[0]
    nk = PEER_N_KEYS
    big = jax.ShapeDtypeStruct((PEER_HEADS, nk, t), F32)
    big_spec = pl.BlockSpec((1, nk, tt), lambda i, h: (h, 0, i))
    return pl.pallas_call(
        _select_kernel,
        grid=(t // tt, PEER_HEADS),
        in_specs=[
            pl.BlockSpec((tt, 2 * HEAD_DIM), lambda i, h: (i, h)),
            pl.BlockSpec((1, 2, nk, HEAD_DIM), lambda i, h: (h, 0, 0, 0)),
        ],
        out_specs=[big_spec, big_spec, big_spec, big_spec, pl.BlockSpec((1, 1, tt), lambda i, h: (h, 0, i))],
        out_shape=[big, big, big, big, jax.ShapeDtypeStruct((PEER_HEADS, 1, t), F32)],
        scratch_shapes=[pltpu.VMEM((PEER_TOPK, tt), F32), pltpu.VMEM((PEER_TOPK, tt), F32),
                        pltpu.VMEM((80, tt), F32), pltpu.VMEM((PEER_TOPK, tt), F32)],
        compiler_params=_params(("arbitrary", "arbitrary")),
    )(q, subkeys)


def _peer_kernel(h_ref, u_ref, vt_ref, s1a_ref, e1a_ref, s1b_ref, e1b_ref, s2_ref, e2_ref, thr_ref, o_ref,
                 a0_scr, a1_scr, w0_scr, w1_scr, acc_scr):
    k = pl.program_id(1)
    te, tt = a0_scr.shape
    nk = PEER_N_KEYS
    ni = te // nk

    @pl.when(k == 0)
    def _():
        acc_scr[...] = jnp.zeros_like(acc_scr)
        a1_scr[...] = jnp.zeros_like(a1_scr)
        w0_scr[...] = jnp.zeros_like(w0_scr)
        w1_scr[...] = jnp.zeros_like(w1_scr)

    def gate_stage(a_scr, w_scr, s1_ref, e1_ref, row0):
        for tg in range(tt // 128):
            lanes = slice(tg * 128, (tg + 1) * 128)
            for jb in range(nk // GATE_ROWS):
                keys = slice(jb * GATE_ROWS, (jb + 1) * GATE_ROWS)
                gates = [None] * ni
                for hd in range(PEER_HEADS):
                    s2 = s2_ref[hd, keys, lanes]
                    e2 = e2_ref[hd, keys, lanes]
                    thr = thr_ref[hd, :, lanes]
                    for il in range(ni):
                        r = row0 + il
                        cand = s1_ref[hd, r:r + 1, lanes] + s2
                        sel = jnp.where(cand >= thr, e1_ref[hd, r:r + 1, lanes] * e2, 0.0)
                        gates[il] = sel if gates[il] is None else gates[il] + sel
                for il in range(ni):
                    rows = slice(il * nk + jb * GATE_ROWS, il * nk + (jb + 1) * GATE_ROWS)
                    a = a_scr[rows, lanes]
                    act = 0.5 * a * (1.0 + lax.erf(a * (2.0 ** -0.5)))
                    w_scr[rows, lanes] = (gates[il] * act).astype(BF16)

    def stage(u, vt, a_new, a_old, w_new, w_old, s1_ref, e1_ref, row0):
        a_new[...] = _dot_nt(u, h_ref[...])
        gate_stage(a_old, w_new, s1_ref, e1_ref, row0)
        return _dot(vt, w_old[...])

    ya = stage(u_ref[:te, :], vt_ref[:, :te], a0_scr, a1_scr, w1_scr, w0_scr, s1a_ref, e1a_ref, ni)
    yb = stage(u_ref[te:, :], vt_ref[:, te:], a1_scr, a0_scr, w0_scr, w1_scr, s1b_ref, e1b_ref, 0)
    acc_scr[...] += ya + yb

    @pl.when(k == pl.num_programs(1) - 1)
    def _():
        o_ref[...] = acc_scr[...].T


PEER_TE = 512


def _peer_experts(h, u, vt, s1, e1, s2, e2, thr, *, tt=512):
    t, d = h.shape
    n_exp = u.shape[0]
    nk = PEER_N_KEYS
    te = PEER_TE
    nb = n_exp // (2 * te)
    last = nb - 1

    def row_spec(shift):
        return pl.BlockSpec((PEER_HEADS, 2 * te // nk, tt),
                            lambda i, k: (0, jnp.clip(k + shift, 0, last), i))

    col_spec = pl.BlockSpec((PEER_HEADS, nk, tt), lambda i, k: (0, 0, i))
    return pl.pallas_call(
        _peer_kernel,
        grid=(t // tt, nb + 1),
        in_specs=[
            pl.BlockSpec((tt, d), lambda i, k: (i, 0)),
            pl.BlockSpec((2 * te, d), lambda i, k: (jnp.minimum(k, last), 0)),
            pl.BlockSpec((d, 2 * te), lambda i, k: (0, jnp.maximum(k - 1, 0))),
            row_spec(-1), row_spec(-1), row_spec(0), row_spec(0), col_spec, col_spec,
            pl.BlockSpec((PEER_HEADS, 1, tt), lambda i, k: (0, 0, i)),
        ],
        out_specs=pl.BlockSpec((tt, d), lambda i, k: (i, 0)),
        out_shape=jax.ShapeDtypeStruct((t, d), F32),
        scratch_shapes=[pltpu.VMEM((te, tt), F32), pltpu.VMEM((te, tt), F32),
                        pltpu.VMEM((te, tt), BF16), pltpu.VMEM((te, tt), BF16), pltpu.VMEM((d, tt), F32)],
        compiler_params=_params(("arbitrary", "arbitrary")),
    )(h, u, vt, s1, e1, s1, e1, s2, e2, thr)


def _rope_tables(seq_len):
    t = jnp.arange(seq_len, dtype=jnp.int32)
    row = (t // GRID_W).astype(F32)
    col = (t % GRID_W).astype(F32)
    n_freq = HEAD_DIM // 4
    inv_freq = ROPE_THETA ** (-jnp.arange(n_freq, dtype=F32) / n_freq)
    ang_r = row[:, None] * inv_freq[None, :]
    ang_c = col[:, None] * inv_freq[None, :]
    ang = jnp.concatenate([ang_r, ang_r, ang_c, ang_c], axis=-1)
    sign = np.tile(np.repeat(np.array([-1.0, 1.0], np.float32), n_freq), 2)
    return jnp.cos(ang), jnp.sin(ang) * sign


def _col_flags(sizes_and_flags):
    out = []
    for size, fl in sizes_and_flags:
        out += [fl] * (size // FLAG_GROUP)
    return jnp.asarray(out, jnp.int32)


def kernel(x, c, ctx, c_ctx, mod_w, mod_b, ln_g, ln_b, ab_w_in, ab_w_out, a_sink, b_rpb, c_w_in, c_w_out,
           c_q_gain, c_k_gain, peer_wq, peer_subkeys, peer_u, peer_v):
    n_batch, seq_len, d = x.shape
    ctx_len = ctx.shape[1]
    n_layers = mod_w.shape[0]
    n_lat = n_batch * seq_len
    n_ctx = n_batch * ctx_len
    n_tok = n_lat + n_ctx
    assert seq_len % ROW_TILE == 0 and n_ctx % ROW_TILE == 0 and seq_len % GRID_W == 0
    geo = dict(seq_len=seq_len, n_lat_rows=n_lat)

    xs = jnp.concatenate([x.reshape(n_lat, d), ctx.reshape(n_ctx, d)], axis=0)
    cos, sin_signed = _rope_tables(seq_len)

    c_rows = jnp.concatenate([c, c_ctx[None, :], jnp.zeros((8 - n_batch - 1, d), F32)], axis=0)
    mod = _modulation(c_rows, mod_w, mod_b).reshape(n_layers, 8, 6, d)

    def mod_vec(layer, k):
        return mod[layer, :n_batch + 1, k, :].reshape(n_batch + 1, 1, d)

    a_q, a_kv, b_w = A_Q_HEADS * HEAD_DIM, A_KV_HEADS * HEAD_DIM, B_HEADS * HEAD_DIM
    ab_flags = _col_flags([(a_q, FLAG_ROPE | FLAG_SCALE), (a_kv, FLAG_ROPE), (a_kv, 0),
                           (b_w, FLAG_SCALE), (b_w, 0), (b_w, 0)])
    c_q, c_kv = C_Q_HEADS * HEAD_DIM, C_KV_HEADS * HEAD_DIM
    c_flags = _col_flags([(c_q, FLAG_RMS_Q | FLAG_ROPE | FLAG_SCALE), (c_kv, FLAG_RMS_K | FLAG_ROPE), (c_kv, 0)])
    plain_flags = _col_flags([(peer_wq.shape[2], 0)])
    no_gain = jnp.zeros((8, HEAD_DIM), F32)

    for layer in range(n_layers):
        last = layer == n_layers - 1
        i = layer // 2
        sh1, sc1, g1, sh2, sc2, g2 = (mod_vec(layer, k) for k in range(6))

        if layer % 2 == 0:
            p = _projection(xs, sc1, sh1, ab_w_in[i].astype(BF16), ab_flags, cos, sin_signed, no_gain,
                            out_dtype=BF16, **geo)
            ka0, va0 = a_q, a_q + a_kv
            qb0 = a_q + 2 * a_kv
            kb0, vb0 = qb0 + b_w, qb0 + 2 * b_w
            sink = a_sink[i].astype(F32) * LOG2E
            out_a = _window_attention(p, sink, n_batch=n_batch, seq_len=seq_len, ctx_len=ctx_len,
                                      k_col0=ka0, v_col0=va0, total_rows=n_tok)
            out_b = _neighborhood_attention(p, _neighborhood_bias(b_rpb[i]), n_batch=n_batch, seq_len=seq_len,
                                            ctx_len=ctx_len, q_col0=qb0, k_col0=kb0, v_col0=vb0, total_rows=n_tok)
            out_a = _flash(p, sink, n_batch=n_batch, q_rows=ctx_len, q_row0=n_lat, q_col0=0,
                           kv_heads=A_KV_HEADS, group=A_GROUP, k_col0=ka0, v_col0=va0,
                           ctx_row0=n_lat, ctx_len=ctx_len, lat_len=0, tq=ctx_len, total_rows=n_tok, out=out_a)
            out_b = _flash(p, None, n_batch=n_batch, q_rows=ctx_len, q_row0=n_lat, q_col0=qb0,
                           kv_heads=B_HEADS, group=1, k_col0=kb0, v_col0=vb0,
                           ctx_row0=n_lat, ctx_len=ctx_len, lat_len=0, tq=ctx_len, total_rows=n_tok, out=out_b)
            w_out = ab_w_out[i].astype(BF16)
            mm = [(out_a, w_out[:a_q]), (out_b, w_out[a_q:])]
        else:
            gains = jnp.concatenate([c_q_gain[i][None], c_k_gain[i][None], jnp.zeros((6, HEAD_DIM), F32)], axis=0)
            p = _projection(xs, sc1, sh1, c_w_in[i].astype(BF16), c_flags, cos, sin_signed, gains,
                            out_dtype=BF16, **geo)
            kc0, vc0 = c_q, c_q + c_kv
            out_c = _flash(p, None, n_batch=n_batch, q_rows=seq_len, q_row0=0, q_col0=0,
                           kv_heads=C_KV_HEADS, group=C_GROUP, k_col0=kc0, v_col0=vc0,
                           ctx_row0=n_lat, ctx_len=ctx_len, lat_len=seq_len, tq=256, total_rows=n_tok, sub=128)
            out_c = _flash(p, None, n_batch=n_batch, q_rows=ctx_len, q_row0=n_lat, q_col0=0,
                           kv_heads=C_KV_HEADS, group=C_GROUP, k_col0=kc0, v_col0=vc0,
                           ctx_row0=n_lat, ctx_len=ctx_len, lat_len=0, tq=ctx_len, total_rows=n_tok, out=out_c)
            mm = [(out_c, c_w_out[i].astype(BF16))]

        xs = _residual_ln(xs, g1, ln_g[layer, 0], ln_b[layer, 0], mm=mm, **geo)

        q, h = _projection(xs, sc2, sh2, peer_wq[layer].astype(BF16), plain_flags, cos, sin_signed, no_gain,
                           out_dtype=F32, emit_h=True, **geo)
        s1, e1, s2, e2, thr = _peer_select(q, peer_subkeys[layer].astype(BF16))
        y = _peer_experts(h, peer_u[layer].astype(BF16), peer_v[layer].astype(BF16).T, s1, e1, s2, e2, thr)
        xs = _residual_ln(xs, g2, ln_g[layer, 1], ln_b[layer, 1], y=y,
                          n_out_rows=n_lat if last else None, **geo)

    return xs.reshape(n_batch, seq_len, d)
```

```python
import functools

import numpy as np
import jax
import jax.numpy as jnp
from jax import lax
from jax.experimental import pallas as pl
from jax.experimental.pallas import tpu as pltpu

F32 = jnp.float32
BF16 = jnp.bfloat16

DEPTH = 4
GRID_W = 64
HEAD_DIM = 128
ATTN_SCALE = HEAD_DIM ** -0.5
LOG2E = 1.4426950408889634
Q_SCALE = ATTN_SCALE * LOG2E
ROPE_THETA = 10000.0
NEG_INF = -1e30

A_Q_HEADS = 8
A_KV_HEADS = 2
A_GROUP = A_Q_HEADS // A_KV_HEADS
A_WINDOW = 128
B_HEADS = 8
B_WIN_ROWS = 8
B_WIN_COLS = 16
NA_ROWS = 4
C_Q_HEADS = 16
C_KV_HEADS = 4
C_GROUP = C_Q_HEADS // C_KV_HEADS

PEER_HEADS = 8
PEER_N_KEYS = 128
PEER_TOPK = 16
GATE_ROWS = 32
DENSE_TK = 512
CHUNKS_PER_TRIP = 2

DEEPNORM_ALPHA = (2 * DEPTH) ** 0.25

VMEM_LIMIT_BYTES = 56 * 1024 * 1024

ROW_TILE = 512
COL_TILES = (1536, 1024, 512, 256)
FLAG_GROUP = 256


def _dot(a, b):
    return jnp.dot(a, b, preferred_element_type=F32)


def _dot_nt(a, b):
    return lax.dot_general(a, b, (((1,), (1,)), ((), ())), preferred_element_type=F32)


def _params(sem, vmem=None):
    return pltpu.CompilerParams(dimension_semantics=sem, vmem_limit_bytes=vmem or VMEM_LIMIT_BYTES)


def _mod_kernel(c_ref, w_ref, b_ref, o_ref):
    c = c_ref[...]
    a = c * (1.0 / (1.0 + jnp.exp(-c)))
    o_ref[0] = _dot(a.astype(BF16), w_ref[0].astype(BF16)) + b_ref[0]


def _modulation(c_rows, mod_w, mod_b):
    n_layers, d, n = mod_w.shape
    tn = 1024
    rows = c_rows.shape[0]
    return pl.pallas_call(
        _mod_kernel,
        grid=(n_layers, n // tn),
        in_specs=[
            pl.BlockSpec((rows, d), lambda l, j: (0, 0)),
            pl.BlockSpec((1, d, tn), lambda l, j: (l, 0, j)),
            pl.BlockSpec((1, 1, tn), lambda l, j: (l, 0, j)),
        ],
        out_specs=pl.BlockSpec((1, rows, tn), lambda l, j: (l, 0, j)),
        out_shape=jax.ShapeDtypeStruct((n_layers, rows, n), F32),
        compiler_params=_params(("arbitrary", "arbitrary")),
    )(c_rows, mod_w, mod_b.reshape(n_layers, 1, n))


FLAG_RMS_Q = 1
FLAG_RMS_K = 2
FLAG_ROPE = 4
FLAG_SCALE = 8


def _proj_kernel(flags_ref, x_ref, sc_ref, sh_ref, w_ref, cos_ref, sin_ref, gain_ref, o_ref, *rest,
                 n_lat_tiles, emit_h):
    if emit_h:
        h_out_ref, h_scr, acc_scr = rest
    else:
        h_scr, acc_scr = rest
    i = pl.program_id(0)
    j = pl.program_id(1)
    tn = o_ref.shape[1]

    @pl.when(j == 0)
    def _():
        h = (x_ref[...] * (1.0 + sc_ref[0]) + sh_ref[0]).astype(BF16)
        h_scr[...] = h
        if emit_h:
            h_out_ref[...] = h

    acc_scr[...] = _dot(h_scr[...], w_ref[...])
    is_lat = i < n_lat_tiles
    lane = lax.broadcasted_iota(jnp.int32, (1, HEAD_DIM), 1)
    odd_seg = ((lane // (HEAD_DIM // 4)) % 2) == 1

    for g in range(tn // FLAG_GROUP):
        fl = flags_ref[j * (tn // FLAG_GROUP) + g]
        rms = fl & 3
        do_rope = jnp.logical_and((fl & FLAG_ROPE) != 0, is_lat)
        scale = jnp.where((fl & FLAG_SCALE) != 0, Q_SCALE, 1.0).astype(F32)
        for hh in range(FLAG_GROUP // HEAD_DIM):
            c0 = g * FLAG_GROUP + hh * HEAD_DIM
            cols = slice(c0, c0 + HEAD_DIM)

            @pl.when(rms != 0)
            def _():
                y = acc_scr[:, cols]
                gain = gain_ref[pl.ds(rms - 1, 1), :]
                y = y * lax.rsqrt(jnp.mean(y * y, axis=-1, keepdims=True) + 1e-6)
                acc_scr[:, cols] = y * gain

            @pl.when(do_rope)
            def _():
                y = acc_scr[:, cols]
                r_dn = pltpu.roll(y, HEAD_DIM // 4, 1)
                r_up = pltpu.roll(y, 3 * HEAD_DIM // 4, 1)
                y = y * cos_ref[...] + jnp.where(odd_seg, r_dn, r_up) * sin_ref[...]
                o_ref[:, cols] = (y * scale).astype(o_ref.dtype)

            @pl.when(jnp.logical_not(do_rope))
            def _():
                o_ref[:, cols] = (acc_scr[:, cols] * scale).astype(o_ref.dtype)


def _projection(x, sc, sh, w, flags, cos, sin_signed, gains, *, seq_len, n_lat_rows, out_dtype, emit_h=False):
    t, d = x.shape
    n = w.shape[1]
    tm = ROW_TILE
    tn = next(c for c in COL_TILES if n % c == 0)
    n_lat_tiles = n_lat_rows // tm
    pos_tiles = seq_len // tm
    n_batch = n_lat_rows // seq_len

    def mod_map(i, j, fl):
        return (jnp.minimum((i * tm) // seq_len, n_batch), 0, 0)

    out_shape = [jax.ShapeDtypeStruct((t, n), out_dtype)]
    out_specs = [pl.BlockSpec((tm, tn), lambda i, j, fl: (i, j))]
    if emit_h:
        out_shape.append(jax.ShapeDtypeStruct((t, d), BF16))
        out_specs.append(pl.BlockSpec((tm, d), lambda i, j, fl: (i, 0)))
    grid_spec = pltpu.PrefetchScalarGridSpec(
        num_scalar_prefetch=1,
        grid=(t // tm, n // tn),
        in_specs=[
            pl.BlockSpec((tm, d), lambda i, j, fl: (i, 0)),
            pl.BlockSpec((1, 1, d), mod_map),
            pl.BlockSpec((1, 1, d), mod_map),
            pl.BlockSpec((d, tn), lambda i, j, fl: (0, j)),
            pl.BlockSpec((tm, HEAD_DIM), lambda i, j, fl: (i % pos_tiles, 0)),
            pl.BlockSpec((tm, HEAD_DIM), lambda i, j, fl: (i % pos_tiles, 0)),
            pl.BlockSpec((8, HEAD_DIM), lambda i, j, fl: (0, 0)),
        ],
        out_specs=out_specs,
        scratch_shapes=[pltpu.VMEM((tm, d), BF16), pltpu.VMEM((tm, tn), F32)],
    )
    res = pl.pallas_call(
        functools.partial(_proj_kernel, n_lat_tiles=n_lat_tiles, emit_h=emit_h),
        grid_spec=grid_spec,
        out_shape=out_shape,
        compiler_params=_params(("arbitrary", "arbitrary")),
    )(flags, x, sc, sh, w, cos, sin_signed, gains)
    return res if emit_h else res[0]


def _ctx_attn_kernel(sink_ref, q_ref, k_ref, v_ref, o_ref, *, group, has_sink):
    tq = q_ref.shape[0]
    q = jnp.concatenate([q_ref[:, hh * HEAD_DIM:(hh + 1) * HEAD_DIM] for hh in range(group)], axis=0)
    s = _dot_nt(q, k_ref[...])
    m = jnp.max(s, axis=-1, keepdims=True)
    if has_sink:
        kvh = pl.program_id(1)
        sk = jnp.concatenate(
            [jnp.full((tq, 1), sink_ref[kvh * group + hh], F32) for hh in range(group)], axis=0)
        m = jnp.maximum(m, sk)
    p = jnp.exp2(s - m)
    l = jnp.sum(p, axis=-1, keepdims=True)
    if has_sink:
        l = l + jnp.exp2(sk - m)
    o = _dot(p.astype(BF16), v_ref[...]) / l
    for hh in range(group):
        o_ref[:, hh * HEAD_DIM:(hh + 1) * HEAD_DIM] = o[hh * tq:(hh + 1) * tq].astype(o_ref.dtype)


def _ctx_attention(p, sink, *, n_batch, ctx_row0, ctx_len, q_col0, kv_heads, group, k_col0, v_col0):
    has_sink = sink is not None
    if sink is None:
        sink = jnp.zeros((kv_heads * group,), F32)
    gw = group * HEAD_DIM
    kb, vb, qb = k_col0 // HEAD_DIM, v_col0 // HEAD_DIM, q_col0 // gw
    c0b = ctx_row0 // ctx_len
    return pl.pallas_call(
        functools.partial(_ctx_attn_kernel, group=group, has_sink=has_sink),
        grid=(n_batch, kv_heads),
        in_specs=[
            pl.BlockSpec(memory_space=pltpu.SMEM),
            pl.BlockSpec((ctx_len, gw), lambda b, h: (c0b + b, qb + h)),
            pl.BlockSpec((ctx_len, HEAD_DIM), lambda b, h: (c0b + b, kb + h)),
            pl.BlockSpec((ctx_len, HEAD_DIM), lambda b, h: (c0b + b, vb + h)),
        ],
        out_specs=pl.BlockSpec((ctx_len, gw), lambda b, h: (b, h)),
        out_shape=jax.ShapeDtypeStruct((n_batch * ctx_len, kv_heads * gw), BF16),
        compiler_params=_params(("arbitrary", "arbitrary")),
    )(sink, p, p, p)


def _dense_attn_kernel(q_ref, kc_ref, vtc_ref, kl_ref, vtl_ref, o_ref, m_scr, l_scr, acc_scr, *, group, lanes):
    tq = q_ref.shape[0]
    n_q = group * tq
    n_chunks = vtl_ref.shape[0]
    tk = vtl_ref.shape[2]
    q = jnp.concatenate([q_ref[:, hh * HEAD_DIM:(hh + 1) * HEAD_DIM] for hh in range(group)], axis=0)
    q_groups = [q[g * lanes:(g + 1) * lanes] for g in range(n_q // lanes)]

    def softmax_step(state, s, vt):
        mx = jnp.max(s, axis=0, keepdims=True)
        if state is None:
            m_new = mx
        else:
            m_old, l_old, acc_old = state
            m_new = jnp.maximum(m_old, mx)
            a = jnp.exp2(m_old - m_new)
        p = jnp.exp2(s - m_new)
        psum = jnp.sum(p, axis=0, keepdims=True)
        pv = _dot(vt, p.astype(BF16))
        if state is None:
            return m_new, psum, pv
        return m_new, a * l_old + psum, a * acc_old + pv

    def update(chunks, first):
        scores = [[_dot_nt(k, qg) for qg in q_groups] for k, _ in chunks]
        states = []
        for g in range(len(q_groups)):
            cols = slice(g * lanes, (g + 1) * lanes)
            states.append(None if first else (m_scr[:, cols], l_scr[:, cols], acc_scr[:, cols]))
        for ci, (_, vt) in enumerate(chunks):
            states = [softmax_step(states[g], scores[ci][g], vt) for g in range(len(q_groups))]
        for g, (m, l, acc) in enumerate(states):
            cols = slice(g * lanes, (g + 1) * lanes)
            m_scr[:, cols] = m
            l_scr[:, cols] = l
            acc_scr[:, cols] = acc

    update([(kc_ref[...], vtc_ref[0])], True)

    def body(c, carry):
        chunks = []
        for half in range(CHUNKS_PER_TRIP):
            cc = CHUNKS_PER_TRIP * c + half
            r0 = pl.multiple_of(cc * tk, tk)
            chunks.append((kl_ref[pl.ds(r0, tk), :], vtl_ref[cc]))
        update(chunks, False)
        return carry

    lax.fori_loop(0, n_chunks // CHUNKS_PER_TRIP, body, 0)

    o = (acc_scr[...] / l_scr[...]).T
    for hh in range(group):
        o_ref[:, hh * HEAD_DIM:(hh + 1) * HEAD_DIM] = o[hh * tq:(hh + 1) * tq].astype(o_ref.dtype)


def _dense_attention(p, vt_lat, vt_ctx, *, n_batch, seq_len, ctx_len, kv_heads, group, k_col0, tq=256, lanes=256):
    nq = seq_len // tq
    gw = group * HEAD_DIM
    kb = k_col0 // HEAD_DIM
    c0b = n_batch * seq_len // ctx_len
    n_chunks = vt_lat.shape[0] // n_batch
    tk = vt_lat.shape[2]
    n_q = group * tq
    return pl.pallas_call(
        functools.partial(_dense_attn_kernel, group=group, lanes=lanes),
        grid=(n_batch, kv_heads, nq),
        in_specs=[
            pl.BlockSpec((tq, gw), lambda b, h, i: (b * nq + i, h)),
            pl.BlockSpec((ctx_len, HEAD_DIM), lambda b, h, i: (c0b + b, kb + h)),
            pl.BlockSpec((1, HEAD_DIM, ctx_len), lambda b, h, i: (b, h, 0)),
            pl.BlockSpec((seq_len, HEAD_DIM), lambda b, h, i: (b, kb + h)),
            pl.BlockSpec((n_chunks, HEAD_DIM, tk), lambda b, h, i: (b, h, 0)),
        ],
        out_specs=pl.BlockSpec((tq, gw), lambda b, h, i: (b * nq + i, h)),
        out_shape=jax.ShapeDtypeStruct((n_batch * seq_len, kv_heads * gw), BF16),
        scratch_shapes=[pltpu.VMEM((1, n_q), F32), pltpu.VMEM((1, n_q), F32), pltpu.VMEM((HEAD_DIM, n_q), F32)],
        compiler_params=_params(("arbitrary", "arbitrary", "arbitrary")),
    )(p, p, vt_ctx, p, vt_lat)


def _window_kernel(sink_ref, q_ref, k_ref, v_ref, kc_ref, vc_ref, o_ref, *, seq_len):
    kvh = pl.program_id(1)
    iq = pl.program_id(2)
    tq = q_ref.shape[0]
    win = tq + 2 * A_WINDOW
    q0 = iq * tq
    start = pl.multiple_of(jnp.clip(q0 - A_WINDOW, 0, seq_len - win), A_WINDOW)
    q = jnp.concatenate([q_ref[:, hh * HEAD_DIM:(hh + 1) * HEAD_DIM] for hh in range(A_GROUP)], axis=0)
    m_rows = A_GROUP * tq

    s = _dot_nt(q, k_ref[pl.ds(start, win), :])
    qpos = q0 + (lax.broadcasted_iota(jnp.int32, (m_rows, win), 0) & (tq - 1))
    kpos = start + lax.broadcasted_iota(jnp.int32, (m_rows, win), 1)
    s = jnp.where(jnp.abs(kpos - qpos) <= A_WINDOW, s, NEG_INF)
    sc = _dot_nt(q, kc_ref[...])
    sk = jnp.concatenate(
        [jnp.full((tq, 1), sink_ref[kvh * A_GROUP + hh], F32) for hh in range(A_GROUP)], axis=0)
    m = jnp.maximum(jnp.maximum(jnp.max(s, axis=-1, keepdims=True), jnp.max(sc, axis=-1, keepdims=True)), sk)
    p = jnp.exp2(s - m)
    pc = jnp.exp2(sc - m)
    l = jnp.sum(p, axis=-1, keepdims=True) + jnp.sum(pc, axis=-1, keepdims=True) + jnp.exp2(sk - m)
    o = (_dot(p.astype(BF16), v_ref[pl.ds(start, win), :]) + _dot(pc.astype(BF16), vc_ref[...])) / l
    for hh in range(A_GROUP):
        o_ref[:, hh * HEAD_DIM:(hh + 1) * HEAD_DIM] = o[hh * tq:(hh + 1) * tq].astype(o_ref.dtype)


def _window_attention(p, sink, *, n_batch, seq_len, ctx_len, k_col0, v_col0, tq=128):
    nq = seq_len // tq
    gw = A_GROUP * HEAD_DIM
    kb, vb = k_col0 // HEAD_DIM, v_col0 // HEAD_DIM
    c0b = n_batch * seq_len // ctx_len
    return pl.pallas_call(
        functools.partial(_window_kernel, seq_len=seq_len),
        grid=(n_batch, A_KV_HEADS, nq),
        in_specs=[
            pl.BlockSpec(memory_space=pltpu.SMEM),
            pl.BlockSpec((tq, gw), lambda b, h, i: (b * nq + i, h)),
            pl.BlockSpec((seq_len, HEAD_DIM), lambda b, h, i: (b, kb + h)),
            pl.BlockSpec((seq_len, HEAD_DIM), lambda b, h, i: (b, vb + h)),
            pl.BlockSpec((ctx_len, HEAD_DIM), lambda b, h, i: (c0b + b, kb + h)),
            pl.BlockSpec((ctx_len, HEAD_DIM), lambda b, h, i: (c0b + b, vb + h)),
        ],
        out_specs=pl.BlockSpec((tq, gw), lambda b, h, i: (b * nq + i, h)),
        out_shape=jax.ShapeDtypeStruct((n_batch * seq_len, A_KV_HEADS * gw), BF16),
        compiler_params=_params(("arbitrary", "arbitrary", "arbitrary")),
    )(sink, p, p, p, p, p)


def _neighborhood_kernel(q_ref, k_ref, v_ref, kc_ref, vc_ref, bias_ref, o_ref, *, n_rows):
    n_keys = B_WIN_ROWS * GRID_W

    def body(t, carry):
        q0 = pl.multiple_of(t * (NA_ROWS * GRID_W), NA_ROWS * GRID_W)
        q_all = q_ref[pl.ds(q0, NA_ROWS * GRID_W), :]
        sc_all = _dot_nt(q_all, kc_ref[...])
        local = []
        for j in range(NA_ROWS):
            r = t * NA_ROWS + j
            rstart = jnp.clip(r - B_WIN_ROWS // 2, 0, n_rows - B_WIN_ROWS)
            k0 = pl.multiple_of(rstart * GRID_W, GRID_W)
            s = _dot_nt(q_all[j * GRID_W:(j + 1) * GRID_W], k_ref[pl.ds(k0, n_keys), :]) + bias_ref[0, r - rstart]
            local.append((k0, s))
        outs = []
        for j, (k0, s) in enumerate(local):
            sc = sc_all[j * GRID_W:(j + 1) * GRID_W]
            m = jnp.maximum(jnp.max(s, axis=-1, keepdims=True), jnp.max(sc, axis=-1, keepdims=True))
            p = jnp.exp2(s - m)
            pc = jnp.exp2(sc - m)
            l = jnp.sum(p, axis=-1, keepdims=True) + jnp.sum(pc, axis=-1, keepdims=True)
            o = (_dot(p.astype(BF16), v_ref[pl.ds(k0, n_keys), :]) + _dot(pc.astype(BF16), vc_ref[...])) / l
            outs.append(o.astype(o_ref.dtype))
        o_ref[pl.ds(q0, NA_ROWS * GRID_W), :] = jnp.concatenate(outs, axis=0)
        return carry

    lax.fori_loop(0, n_rows // NA_ROWS, body, 0)


def _neighborhood_bias(rpb):
    qc = np.arange(GRID_W)[:, None]
    kc = np.arange(GRID_W)[None, :]
    wstart = np.clip(qc - B_WIN_COLS // 2, 0, GRID_W - B_WIN_COLS)
    col_ok = (kc >= wstart) & (kc < wstart + B_WIN_COLS)
    dcol = np.clip(kc - qc + B_WIN_COLS - 1, 0, 2 * B_WIN_COLS - 2)
    d = np.arange(B_WIN_ROWS)[:, None]
    kk = np.arange(B_WIN_ROWS)[None, :]
    drow = kk - d + B_WIN_ROWS - 1
    bias = (rpb.astype(F32) * LOG2E)[:, drow][:, :, :, dcol]
    bias = jnp.where(col_ok[None, None, None], bias, NEG_INF)
    bias = jnp.transpose(bias, (0, 1, 3, 2, 4))
    return bias.reshape(rpb.shape[0], B_WIN_ROWS, GRID_W, B_WIN_ROWS * GRID_W)


def _neighborhood_attention(p, bias, *, n_batch, seq_len, ctx_len, q_col0, k_col0, v_col0):
    qb, kb, vb = q_col0 // HEAD_DIM, k_col0 // HEAD_DIM, v_col0 // HEAD_DIM
    c0b = n_batch * seq_len // ctx_len
    n_rows = seq_len // GRID_W
    return pl.pallas_call(
        functools.partial(_neighborhood_kernel, n_rows=n_rows),
        grid=(n_batch, B_HEADS),
        in_specs=[
            pl.BlockSpec((seq_len, HEAD_DIM), lambda b, h: (b, qb + h)),
            pl.BlockSpec((seq_len, HEAD_DIM), lambda b, h: (b, kb + h)),
            pl.BlockSpec((seq_len, HEAD_DIM), lambda b, h: (b, vb + h)),
            pl.BlockSpec((ctx_len, HEAD_DIM), lambda b, h: (c0b + b, kb + h)),
            pl.BlockSpec((ctx_len, HEAD_DIM), lambda b, h: (c0b + b, vb + h)),
            pl.BlockSpec((1, B_WIN_ROWS, GRID_W, B_WIN_ROWS * GRID_W), lambda b, h: (h, 0, 0, 0)),
        ],
        out_specs=pl.BlockSpec((seq_len, HEAD_DIM), lambda b, h: (b, h)),
        out_shape=jax.ShapeDtypeStruct((n_batch * seq_len, B_HEADS * HEAD_DIM), BF16),
        compiler_params=_params(("arbitrary", "arbitrary")),
    )(p, p, p, p, p, bias)


def _residual_ln_kernel(*refs, n_mm, n_lat_tiles):
    if n_mm:
        lat_refs = refs[:n_mm]
        ctx_refs = refs[n_mm:2 * n_mm]
        w_refs = refs[2 * n_mm:3 * n_mm]
        x_ref, g_ref, lg_ref, lb_ref, o_ref = refs[3 * n_mm:]
        is_lat = pl.program_id(0) < n_lat_tiles
        y = None
        for lat_ref, ctx_ref, w_ref in zip(lat_refs, ctx_refs, w_refs):
            a = jnp.where(is_lat, lat_ref[...], ctx_ref[...])
            yi = _dot(a, w_ref[...])
            y = yi if y is None else y + yi
    else:
        y_ref, x_ref, g_ref, lg_ref, lb_ref, o_ref = refs
        y = y_ref[...]
    z = DEEPNORM_ALPHA * x_ref[...] + g_ref[0] * y
    mu = jnp.mean(z, axis=-1, keepdims=True)
    zc = z - mu
    var = jnp.mean(zc * zc, axis=-1, keepdims=True)
    o_ref[...] = (zc * lax.rsqrt(var + 1e-5)) * lg_ref[...] + lb_ref[...]


def _residual_ln(x, gate, ln_g, ln_b, *, seq_len, n_lat_rows, mm=(), y=None, n_out_rows=None, tm=256):
    t, d = x.shape
    n_out_rows = n_out_rows or t
    n_batch = n_lat_rows // seq_len

    def mod_map(i):
        return (jnp.minimum((i * tm) // seq_len, n_batch), 0, 0)

    n_lat_tiles = n_lat_rows // tm
    n_ctx_tiles = (t - n_lat_rows) // tm
    in_specs, args = [], []
    for a_lat, _, _ in mm:
        in_specs.append(pl.BlockSpec((tm, a_lat.shape[1]), lambda i: (jnp.minimum(i, n_lat_tiles - 1), 0)))
        args.append(a_lat)
    for _, a_ctx, _ in mm:
        in_specs.append(pl.BlockSpec((tm, a_ctx.shape[1]),
                                     lambda i: (jnp.clip(i - n_lat_tiles, 0, n_ctx_tiles - 1), 0)))
        args.append(a_ctx)
    for _, _, w in mm:
        in_specs.append(pl.BlockSpec(w.shape, lambda i: (0, 0)))
        args.append(w)
    if not mm:
        in_specs.append(pl.BlockSpec((tm, d), lambda i: (i, 0)))
        args.append(y)
    in_specs += [
        pl.BlockSpec((tm, d), lambda i: (i, 0)),
        pl.BlockSpec((1, 1, d), mod_map),
        pl.BlockSpec((1, d), lambda i: (0, 0)),
        pl.BlockSpec((1, d), lambda i: (0, 0)),
    ]
    args += [x, gate, ln_g.reshape(1, d), ln_b.reshape(1, d)]
    return pl.pallas_call(
        functools.partial(_residual_ln_kernel, n_mm=len(mm), n_lat_tiles=n_lat_rows // tm),
        grid=(n_out_rows // tm,),
        in_specs=in_specs,
        out_specs=pl.BlockSpec((tm, d), lambda i: (i, 0)),
        out_shape=jax.ShapeDtypeStruct((n_out_rows, d), F32),
        compiler_params=_params(("arbitrary",)),
    )(*args)


def _top_values(x, out_scr):
    rows = x.shape[0]
    ridx = lax.broadcasted_iota(jnp.int32, x.shape, 0).astype(F32)
    for k in range(PEER_TOPK):
        m = jnp.max(x, axis=0, keepdims=True)
        out_scr[k:k + 1, :] = m
        first = jnp.min(jnp.where(x == m, ridx, float(rows)), axis=0, keepdims=True)
        x = jnp.where(ridx == first, -jnp.inf, x)


def _select_kernel(q_ref, sk_ref, s1_ref, e1_ref, s2_ref, e2_ref, thr_ref, t1_scr, t2_scr, cand_scr, best_scr):
    q = q_ref[...].astype(BF16)
    s1 = _dot_nt(sk_ref[0, 0], q[:, :HEAD_DIM])
    s2 = _dot_nt(sk_ref[0, 1], q[:, HEAD_DIM:])
    _top_values(s1, t1_scr)
    _top_values(s2, t2_scr)
    cand_scr[0:16, :] = t1_scr[0:1, :] + t2_scr[0:16, :]
    sub = lax.broadcasted_iota(jnp.int32, (8, 1), 0)
    for a in range(1, 8):
        c = t1_scr[a:a + 1, :] + t2_scr[0:8, :]
        cand_scr[8 + 8 * a:16 + 8 * a, :] = jnp.where(sub < PEER_TOPK // (a + 1), c, -jnp.inf)
    cand_scr[72:80, :] = t1_scr[8:16, :] + t2_scr[0:1, :]
    _top_values(cand_scr[...], best_scr)
    best = best_scr[...]
    z = jnp.sum(jnp.exp(best - best[0:1, :]), axis=0, keepdims=True)
    s1_ref[0] = s1
    s2_ref[0] = s2
    e1_ref[0] = jnp.exp(s1 - t1_scr[0:1, :]) / z
    e2_ref[0] = jnp.exp(s2 - t2_scr[0:1, :])
    thr_ref[0] = best[PEER_TOPK - 1:PEER_TOPK, :]


def _peer_select(q, subkeys, tt=256):
    t = q.shape[0]
    nk = PEER_N_KEYS
    big = jax.ShapeDtypeStruct((PEER_HEADS, nk, t), F32)
    big_spec = pl.BlockSpec((1, nk, tt), lambda i, h: (h, 0, i))
    return pl.pallas_call(
        _select_kernel,
        grid=(t // tt, PEER_HEADS),
        in_specs=[
            pl.BlockSpec((tt, 2 * HEAD_DIM), lambda i, h: (i, h)),
            pl.BlockSpec((1, 2, nk, HEAD_DIM), lambda i, h: (h, 0, 0, 0)),
        ],
        out_specs=[big_spec, big_spec, big_spec, big_spec, pl.BlockSpec((1, 1, tt), lambda i, h: (h, 0, i))],
        out_shape=[big, big, big, big, jax.ShapeDtypeStruct((PEER_HEADS, 1, t), F32)],
        scratch_shapes=[pltpu.VMEM((PEER_TOPK, tt), F32), pltpu.VMEM((PEER_TOPK, tt), F32),
                        pltpu.VMEM((80, tt), F32), pltpu.VMEM((PEER_TOPK, tt), F32)],
        compiler_params=_params(("arbitrary", "arbitrary")),
    )(q, subkeys)


def _peer_kernel(h_ref, u_ref, vt_ref, s1a_ref, e1a_ref, s1b_ref, e1b_ref, s2_ref, e2_ref, thr_ref, o_ref,
                 a0_scr, a1_scr, w0_scr, w1_scr, acc_scr):
    k = pl.program_id(1)
    te, tt = a0_scr.shape
    nk = PEER_N_KEYS
    ni = te // nk

    @pl.when(k == 0)
    def _():
        acc_scr[...] = jnp.zeros_like(acc_scr)
        a1_scr[...] = jnp.zeros_like(a1_scr)
        w0_scr[...] = jnp.zeros_like(w0_scr)
        w1_scr[...] = jnp.zeros_like(w1_scr)

    def gate_stage(a_scr, w_scr, s1_ref, e1_ref, row0):
        for tg in range(tt // 128):
            lanes = slice(tg * 128, (tg + 1) * 128)
            for jb in range(nk // GATE_ROWS):
                keys = slice(jb * GATE_ROWS, (jb + 1) * GATE_ROWS)
                gates = [None] * ni
                for hd in range(PEER_HEADS):
                    s2 = s2_ref[hd, keys, lanes]
                    e2 = e2_ref[hd, keys, lanes]
                    thr = thr_ref[hd, :, lanes]
                    for il in range(ni):
                        r = row0 + il
                        cand = s1_ref[hd, r:r + 1, lanes] + s2
                        sel = jnp.where(cand >= thr, e1_ref[hd, r:r + 1, lanes] * e2, 0.0)
                        gates[il] = sel if gates[il] is None else gates[il] + sel
                for il in range(ni):
                    rows = slice(il * nk + jb * GATE_ROWS, il * nk + (jb + 1) * GATE_ROWS)
                    a = a_scr[rows, lanes]
                    act = 0.5 * a * (1.0 + lax.erf(a * (2.0 ** -0.5)))
                    w_scr[rows, lanes] = (gates[il] * act).astype(BF16)

    def stage(u, vt, a_new, a_old, w_new, w_old, s1_ref, e1_ref, row0):
        a_new[...] = _dot_nt(u, h_ref[...])
        y = _dot(vt, w_old[...])
        gate_stage(a_old, w_new, s1_ref, e1_ref, row0)
        return y

    ya = stage(u_ref[:te, :], vt_ref[:, :te], a0_scr, a1_scr, w1_scr, w0_scr, s1a_ref, e1a_ref, ni)
    yb = stage(u_ref[te:, :], vt_ref[:, te:], a1_scr, a0_scr, w0_scr, w1_scr, s1b_ref, e1b_ref, 0)
    acc_scr[...] += ya + yb

    @pl.when(k == pl.num_programs(1) - 1)
    def _():
        o_ref[...] = acc_scr[...].T


PEER_TE = 512


def _peer_experts(h, u, vt, s1, e1, s2, e2, thr, *, tt=512):
    t, d = h.shape
    n_exp = u.shape[0]
    nk = PEER_N_KEYS
    te = PEER_TE
    nb = n_exp // (2 * te)
    last = nb - 1

    def row_spec(shift):
        return pl.BlockSpec((PEER_HEADS, 2 * te // nk, tt),
                            lambda i, k: (0, jnp.clip(k + shift, 0, last), i))

    col_spec = pl.BlockSpec((PEER_HEADS, nk, tt), lambda i, k: (0, 0, i))
    return pl.pallas_call(
        _peer_kernel,
        grid=(t // tt, nb + 1),
        in_specs=[
            pl.BlockSpec((tt, d), lambda i, k: (i, 0)),
            pl.BlockSpec((2 * te, d), lambda i, k: (jnp.minimum(k, last), 0)),
            pl.BlockSpec((d, 2 * te), lambda i, k: (0, jnp.maximum(k - 1, 0))),
            row_spec(-1), row_spec(-1), row_spec(0), row_spec(0), col_spec, col_spec,
            pl.BlockSpec((PEER_HEADS, 1, tt), lambda i, k: (0, 0, i)),
        ],
        out_specs=pl.BlockSpec((tt, d), lambda i, k: (i, 0)),
        out_shape=jax.ShapeDtypeStruct((t, d), F32),
        scratch_shapes=[pltpu.VMEM((te, tt), F32), pltpu.VMEM((te, tt), F32),
                        pltpu.VMEM((te, tt), BF16), pltpu.VMEM((te, tt), BF16), pltpu.VMEM((d, tt), F32)],
        compiler_params=_params(("arbitrary", "arbitrary")),
    )(h, u, vt, s1, e1, s1, e1, s2, e2, thr)


def _rope_tables(seq_len):
    t = jnp.arange(seq_len, dtype=jnp.int32)
    row = (t // GRID_W).astype(F32)
    col = (t % GRID_W).astype(F32)
    n_freq = HEAD_DIM // 4
    inv_freq = ROPE_THETA ** (-jnp.arange(n_freq, dtype=F32) / n_freq)
    ang_r = row[:, None] * inv_freq[None, :]
    ang_c = col[:, None] * inv_freq[None, :]
    ang = jnp.concatenate([ang_r, ang_r, ang_c, ang_c], axis=-1)
    sign = np.tile(np.repeat(np.array([-1.0, 1.0], np.float32), n_freq), 2)
    return jnp.cos(ang), jnp.sin(ang) * sign


def _col_flags(sizes_and_flags):
    out = []
    for size, fl in sizes_and_flags:
        out += [fl] * (size // FLAG_GROUP)
    return jnp.asarray(out, jnp.int32)


def kernel(x, c, ctx, c_ctx, mod_w, mod_b, ln_g, ln_b, ab_w_in, ab_w_out, a_sink, b_rpb, c_w_in, c_w_out,
           c_q_gain, c_k_gain, peer_wq, peer_subkeys, peer_u, peer_v):
    n_batch, seq_len, d = x.shape
    ctx_len = ctx.shape[1]
    n_layers = mod_w.shape[0]
    n_lat = n_batch * seq_len
    n_ctx = n_batch * ctx_len
    assert seq_len % ROW_TILE == 0 and n_ctx % ROW_TILE == 0 and seq_len % GRID_W == 0
    geo = dict(seq_len=seq_len, n_lat_rows=n_lat)

    xs = jnp.concatenate([x.reshape(n_lat, d), ctx.reshape(n_ctx, d)], axis=0)
    cos, sin_signed = _rope_tables(seq_len)

    c_rows = jnp.concatenate([c, c_ctx[None, :], jnp.zeros((8 - n_batch - 1, d), F32)], axis=0)
    mod = _modulation(c_rows, mod_w, mod_b).reshape(n_layers, 8, 6, d)

    def mod_vec(layer, k):
        return mod[layer, :n_batch + 1, k, :].reshape(n_batch + 1, 1, d)

    a_q, a_kv, b_w = A_Q_HEADS * HEAD_DIM, A_KV_HEADS * HEAD_DIM, B_HEADS * HEAD_DIM
    ab_flags = _col_flags([(a_q, FLAG_ROPE | FLAG_SCALE), (a_kv, FLAG_ROPE), (a_kv, 0),
                           (b_w, FLAG_SCALE), (b_w, 0), (b_w, 0)])
    c_q, c_kv = C_Q_HEADS * HEAD_DIM, C_KV_HEADS * HEAD_DIM
    c_flags = _col_flags([(c_q, FLAG_RMS_Q | FLAG_ROPE | FLAG_SCALE), (c_kv, FLAG_RMS_K | FLAG_ROPE), (c_kv, 0)])
    plain_flags = _col_flags([(peer_wq.shape[2], 0)])
    no_gain = jnp.zeros((8, HEAD_DIM), F32)

    for layer in range(n_layers):
        last = layer == n_layers - 1
        i = layer // 2
        sh1, sc1, g1, sh2, sc2, g2 = (mod_vec(layer, k) for k in range(6))

        if layer % 2 == 0:
            p = _projection(xs, sc1, sh1, ab_w_in[i].astype(BF16), ab_flags, cos, sin_signed, no_gain,
                            out_dtype=BF16, **geo)
            ka0, va0 = a_q, a_q + a_kv
            qb0 = a_q + 2 * a_kv
            kb0, vb0 = qb0 + b_w, qb0 + 2 * b_w
            sink = a_sink[i].astype(F32) * LOG2E
            out_a = _window_attention(p, sink, n_batch=n_batch, seq_len=seq_len, ctx_len=ctx_len,
                                      k_col0=ka0, v_col0=va0)
            out_b = _neighborhood_attention(p, _neighborhood_bias(b_rpb[i]), n_batch=n_batch, seq_len=seq_len,
                                            ctx_len=ctx_len, q_col0=qb0, k_col0=kb0, v_col0=vb0)
            ctx_a = _ctx_attention(p, sink, n_batch=n_batch, ctx_row0=n_lat, ctx_len=ctx_len, q_col0=0,
                                   kv_heads=A_KV_HEADS, group=A_GROUP, k_col0=ka0, v_col0=va0)
            ctx_b = _ctx_attention(p, None, n_batch=n_batch, ctx_row0=n_lat, ctx_len=ctx_len, q_col0=qb0,
                                   kv_heads=B_HEADS, group=1, k_col0=kb0, v_col0=vb0)
            w_out = ab_w_out[i].astype(BF16)
            mm = [(out_a, ctx_a, w_out[:a_q]), (out_b, ctx_b, w_out[a_q:])]
        else:
            gains = jnp.concatenate([c_q_gain[i][None], c_k_gain[i][None], jnp.zeros((6, HEAD_DIM), F32)], axis=0)
            p = _projection(xs, sc1, sh1, c_w_in[i].astype(BF16), c_flags, cos, sin_signed, gains,
                            out_dtype=BF16, **geo)
            kc0, vc0 = c_q, c_q + c_kv
            v_all = p[:, vc0:vc0 + c_kv]
            vt_lat = jnp.swapaxes(v_all[:n_lat].reshape(n_lat // DENSE_TK, DENSE_TK, c_kv), 1, 2)
            vt_ctx = jnp.swapaxes(v_all[n_lat:].reshape(n_batch, ctx_len, c_kv), 1, 2)
            out_c = _dense_attention(p, vt_lat, vt_ctx, n_batch=n_batch, seq_len=seq_len, ctx_len=ctx_len,
                                     kv_heads=C_KV_HEADS, group=C_GROUP, k_col0=kc0)
            ctx_c = _ctx_attention(p, None, n_batch=n_batch, ctx_row0=n_lat, ctx_len=ctx_len, q_col0=0,
                                   kv_heads=C_KV_HEADS, group=C_GROUP, k_col0=kc0, v_col0=vc0)
            mm = [(out_c, ctx_c, c_w_out[i].astype(BF16))]

        xs = _residual_ln(xs, g1, ln_g[layer, 0], ln_b[layer, 0], mm=mm, **geo)

        q, h = _projection(xs, sc2, sh2, peer_wq[layer].astype(BF16), plain_flags, cos, sin_signed, no_gain,
                           out_dtype=F32, emit_h=True, **geo)
        s1, e1, s2, e2, thr = _peer_select(q, peer_subkeys[layer].astype(BF16))
        y = _peer_experts(h, peer_u[layer].astype(BF16), peer_v[layer].astype(BF16).T, s1, e1, s2, e2, thr)
        xs = _residual_ln(xs, g2, ln_g[layer, 1], ln_b[layer, 1], y=y,
                          n_out_rows=n_lat if last else None, **geo)

    return xs.reshape(n_batch, seq_len, d)
```

```python
import functools

import numpy as np
import jax
import jax.numpy as jnp
from jax import lax
from jax.experimental import pallas as pl
from jax.experimental.pallas import tpu as pltpu

F32 = jnp.float32
BF16 = jnp.bfloat16

DEPTH = 4
GRID_W = 64
HEAD_DIM = 128
ATTN_SCALE = HEAD_DIM ** -0.5
LOG2E = 1.4426950408889634
Q_SCALE = ATTN_SCALE * LOG2E
ROPE_THETA = 10000.0
NEG_INF = -1e30

A_Q_HEADS = 8
A_KV_HEADS = 2
A_GROUP = A_Q_HEADS // A_KV_HEADS
A_WINDOW = 128
B_HEADS = 8
B_WIN_ROWS = 8
B_WIN_COLS = 16
NA_ROWS = 4
C_Q_HEADS = 16
C_KV_HEADS = 4
C_GROUP = C_Q_HEADS // C_KV_HEADS

PEER_HEADS = 8
PEER_N_KEYS = 128
PEER_TOPK = 16
GATE_ROWS = 32
DENSE_TK = 512
CHUNKS_PER_TRIP = 2

DEEPNORM_ALPHA = (2 * DEPTH) ** 0.25

VMEM_LIMIT_BYTES = 56 * 1024 * 1024

ROW_TILE = 512
COL_TILES = (1536, 1024, 512, 256)
FLAG_GROUP = 256


def _dot(a, b):
    return jnp.dot(a, b, preferred_element_type=F32)


def _dot_nt(a, b):
    return lax.dot_general(a, b, (((1,), (1,)), ((), ())), preferred_element_type=F32)


def _params(sem, vmem=None):
    return pltpu.CompilerParams(dimension_semantics=sem, vmem_limit_bytes=vmem or VMEM_LIMIT_BYTES)


def _mod_kernel(c_ref, w_ref, b_ref, o_ref):
    c = c_ref[...]
    a = c * (1.0 / (1.0 + jnp.exp(-c)))
    o_ref[0] = _dot(a.astype(BF16), w_ref[0].astype(BF16)) + b_ref[0]


def _modulation(c_rows, mod_w, mod_b):
    n_layers, d, n = mod_w.shape
    tn = 1024
    rows = c_rows.shape[0]
    return pl.pallas_call(
        _mod_kernel,
        grid=(n_layers, n // tn),
        in_specs=[
            pl.BlockSpec((rows, d), lambda l, j: (0, 0)),
            pl.BlockSpec((1, d, tn), lambda l, j: (l, 0, j)),
            pl.BlockSpec((1, 1, tn), lambda l, j: (l, 0, j)),
        ],
        out_specs=pl.BlockSpec((1, rows, tn), lambda l, j: (l, 0, j)),
        out_shape=jax.ShapeDtypeStruct((n_layers, rows, n), F32),
        compiler_params=_params(("arbitrary", "arbitrary")),
    )(c_rows, mod_w, mod_b.reshape(n_layers, 1, n))


FLAG_RMS_Q = 1
FLAG_RMS_K = 2
FLAG_ROPE = 4
FLAG_SCALE = 8


def _proj_kernel(flags_ref, x_ref, sc_ref, sh_ref, w_ref, cos_ref, sin_ref, gain_ref, o_ref, *rest,
                 n_lat_tiles, emit_h):
    if emit_h:
        h_out_ref, h_scr, acc_scr = rest
    else:
        h_scr, acc_scr = rest
    i = pl.program_id(0)
    j = pl.program_id(1)
    tn = o_ref.shape[1]

    @pl.when(j == 0)
    def _():
        h = (x_ref[...] * (1.0 + sc_ref[0]) + sh_ref[0]).astype(BF16)
        h_scr[...] = h
        if emit_h:
            h_out_ref[...] = h

    acc_scr[...] = _dot(h_scr[...], w_ref[...])
    is_lat = i < n_lat_tiles
    lane = lax.broadcasted_iota(jnp.int32, (1, HEAD_DIM), 1)
    odd_seg = ((lane // (HEAD_DIM // 4)) % 2) == 1

    for g in range(tn // FLAG_GROUP):
        fl = flags_ref[j * (tn // FLAG_GROUP) + g]
        rms = fl & 3
        do_rope = jnp.logical_and((fl & FLAG_ROPE) != 0, is_lat)
        scale = jnp.where((fl & FLAG_SCALE) != 0, Q_SCALE, 1.0).astype(F32)
        for hh in range(FLAG_GROUP // HEAD_DIM):
            c0 = g * FLAG_GROUP + hh * HEAD_DIM
            cols = slice(c0, c0 + HEAD_DIM)

            @pl.when(rms != 0)
            def _():
                y = acc_scr[:, cols]
                gain = gain_ref[pl.ds(rms - 1, 1), :]
                y = y * lax.rsqrt(jnp.mean(y * y, axis=-1, keepdims=True) + 1e-6)
                acc_scr[:, cols] = y * gain

            @pl.when(do_rope)
            def _():
                y = acc_scr[:, cols]
                r_dn = pltpu.roll(y, HEAD_DIM // 4, 1)
                r_up = pltpu.roll(y, 3 * HEAD_DIM // 4, 1)
                y = y * cos_ref[...] + jnp.where(odd_seg, r_dn, r_up) * sin_ref[...]
                o_ref[:, cols] = (y * scale).astype(o_ref.dtype)

            @pl.when(jnp.logical_not(do_rope))
            def _():
                o_ref[:, cols] = (acc_scr[:, cols] * scale).astype(o_ref.dtype)


def _projection(x, sc, sh, w, flags, cos, sin_signed, gains, *, seq_len, n_lat_rows, out_dtype, emit_h=False):
    t, d = x.shape
    n = w.shape[1]
    tm = ROW_TILE
    tn = next(c for c in COL_TILES if n % c == 0)
    n_lat_tiles = n_lat_rows // tm
    pos_tiles = seq_len // tm
    n_batch = n_lat_rows // seq_len

    def mod_map(i, j, fl):
        return (jnp.minimum((i * tm) // seq_len, n_batch), 0, 0)

    out_shape = [jax.ShapeDtypeStruct((t, n), out_dtype)]
    out_specs = [pl.BlockSpec((tm, tn), lambda i, j, fl: (i, j))]
    if emit_h:
        out_shape.append(jax.ShapeDtypeStruct((t, d), BF16))
        out_specs.append(pl.BlockSpec((tm, d), lambda i, j, fl: (i, 0)))
    grid_spec = pltpu.PrefetchScalarGridSpec(
        num_scalar_prefetch=1,
        grid=(t // tm, n // tn),
        in_specs=[
            pl.BlockSpec((tm, d), lambda i, j, fl: (i, 0)),
            pl.BlockSpec((1, 1, d), mod_map),
            pl.BlockSpec((1, 1, d), mod_map),
            pl.BlockSpec((d, tn), lambda i, j, fl: (0, j)),
            pl.BlockSpec((tm, HEAD_DIM), lambda i, j, fl: (i % pos_tiles, 0)),
            pl.BlockSpec((tm, HEAD_DIM), lambda i, j, fl: (i % pos_tiles, 0)),
            pl.BlockSpec((8, HEAD_DIM), lambda i, j, fl: (0, 0)),
        ],
        out_specs=out_specs,
        scratch_shapes=[pltpu.VMEM((tm, d), BF16), pltpu.VMEM((tm, tn), F32)],
    )
    res = pl.pallas_call(
        functools.partial(_proj_kernel, n_lat_tiles=n_lat_tiles, emit_h=emit_h),
        grid_spec=grid_spec,
        out_shape=out_shape,
        compiler_params=_params(("arbitrary", "arbitrary")),
    )(flags, x, sc, sh, w, cos, sin_signed, gains)
    return res if emit_h else res[0]


def _ctx_attn_kernel(sink_ref, q_ref, k_ref, v_ref, o_ref, *, group, has_sink):
    tq = q_ref.shape[0]
    q = jnp.concatenate([q_ref[:, hh * HEAD_DIM:(hh + 1) * HEAD_DIM] for hh in range(group)], axis=0)
    s = _dot_nt(q, k_ref[...])
    m = jnp.max(s, axis=-1, keepdims=True)
    if has_sink:
        kvh = pl.program_id(1)
        sk = jnp.concatenate(
            [jnp.full((tq, 1), sink_ref[kvh * group + hh], F32) for hh in range(group)], axis=0)
        m = jnp.maximum(m, sk)
    p = jnp.exp2(s - m)
    l = jnp.sum(p, axis=-1, keepdims=True)
    if has_sink:
        l = l + jnp.exp2(sk - m)
    o = _dot(p.astype(BF16), v_ref[...]) / l
    for hh in range(group):
        o_ref[:, hh * HEAD_DIM:(hh + 1) * HEAD_DIM] = o[hh * tq:(hh + 1) * tq].astype(o_ref.dtype)


def _ctx_attention(p, sink, *, n_batch, ctx_row0, ctx_len, q_col0, kv_heads, group, k_col0, v_col0):
    has_sink = sink is not None
    if sink is None:
        sink = jnp.zeros((kv_heads * group,), F32)
    gw = group * HEAD_DIM
    kb, vb, qb = k_col0 // HEAD_DIM, v_col0 // HEAD_DIM, q_col0 // gw
    c0b = ctx_row0 // ctx_len
    return pl.pallas_call(
        functools.partial(_ctx_attn_kernel, group=group, has_sink=has_sink),
        grid=(n_batch, kv_heads),
        in_specs=[
            pl.BlockSpec(memory_space=pltpu.SMEM),
            pl.BlockSpec((ctx_len, gw), lambda b, h: (c0b + b, qb + h)),
            pl.BlockSpec((ctx_len, HEAD_DIM), lambda b, h: (c0b + b, kb + h)),
            pl.BlockSpec((ctx_len, HEAD_DIM), lambda b, h: (c0b + b, vb + h)),
        ],
        out_specs=pl.BlockSpec((ctx_len, gw), lambda b, h: (b, h)),
        out_shape=jax.ShapeDtypeStruct((n_batch * ctx_len, kv_heads * gw), BF16),
        compiler_params=_params(("arbitrary", "arbitrary")),
    )(sink, p, p, p)


def _dense_attn_kernel(q_ref, kc_ref, vtc_ref, kl_ref, vtl_ref, o_ref, m_scr, l_scr, acc_scr, *, group, lanes):
    tq = q_ref.shape[0]
    n_q = group * tq
    n_chunks = vtl_ref.shape[0]
    tk = vtl_ref.shape[2]
    q = jnp.concatenate([q_ref[:, hh * HEAD_DIM:(hh + 1) * HEAD_DIM] for hh in range(group)], axis=0)
    q_groups = [q[g * lanes:(g + 1) * lanes] for g in range(n_q // lanes)]

    def softmax_step(state, s, vt):
        mx = jnp.max(s, axis=0, keepdims=True)
        if state is None:
            m_new = mx
        else:
            m_old, l_old, acc_old = state
            m_new = jnp.maximum(m_old, mx)
            a = jnp.exp2(m_old - m_new)
        p = jnp.exp2(s - m_new)
        psum = jnp.sum(p, axis=0, keepdims=True)
        pv = _dot(vt, p.astype(BF16))
        if state is None:
            return m_new, psum, pv
        return m_new, a * l_old + psum, a * acc_old + pv

    def update(chunks, first):
        scores = [[_dot_nt(k, qg) for qg in q_groups] for k, _ in chunks]
        states = []
        for g in range(len(q_groups)):
            cols = slice(g * lanes, (g + 1) * lanes)
            states.append(None if first else (m_scr[:, cols], l_scr[:, cols], acc_scr[:, cols]))
        for ci, (_, vt) in enumerate(chunks):
            states = [softmax_step(states[g], scores[ci][g], vt) for g in range(len(q_groups))]
        for g, (m, l, acc) in enumerate(states):
            cols = slice(g * lanes, (g + 1) * lanes)
            m_scr[:, cols] = m
            l_scr[:, cols] = l
            acc_scr[:, cols] = acc

    update([(kc_ref[...], vtc_ref[0])], True)

    def body(c, carry):
        chunks = []
        for half in range(CHUNKS_PER_TRIP):
            cc = CHUNKS_PER_TRIP * c + half
            r0 = pl.multiple_of(cc * tk, tk)
            chunks.append((kl_ref[pl.ds(r0, tk), :], vtl_ref[cc]))
        update(chunks, False)
        return carry

    lax.fori_loop(0, n_chunks // CHUNKS_PER_TRIP, body, 0)

    o = (acc_scr[...] / l_scr[...]).T
    for hh in range(group):
        o_ref[:, hh * HEAD_DIM:(hh + 1) * HEAD_DIM] = o[hh * tq:(hh + 1) * tq].astype(o_ref.dtype)


def _dense_attention(p, vt_lat, vt_ctx, *, n_batch, seq_len, ctx_len, kv_heads, group, k_col0, tq=256, lanes=256):
    nq = seq_len // tq
    gw = group * HEAD_DIM
    kb = k_col0 // HEAD_DIM
    c0b = n_batch * seq_len // ctx_len
    n_chunks = vt_lat.shape[0] // n_batch
    tk = vt_lat.shape[2]
    n_q = group * tq
    return pl.pallas_call(
        functools.partial(_dense_attn_kernel, group=group, lanes=lanes),
        grid=(n_batch, kv_heads, nq),
        in_specs=[
            pl.BlockSpec((tq, gw), lambda b, h, i: (b * nq + i, h)),
            pl.BlockSpec((ctx_len, HEAD_DIM), lambda b, h, i: (c0b + b, kb + h)),
            pl.BlockSpec((1, HEAD_DIM, ctx_len), lambda b, h, i: (b, h, 0)),
            pl.BlockSpec((seq_len, HEAD_DIM), lambda b, h, i: (b, kb + h)),
            pl.BlockSpec((n_chunks, HEAD_DIM, tk), lambda b, h, i: (b, h, 0)),
        ],
        out_specs=pl.BlockSpec((tq, gw), lambda b, h, i: (b * nq + i, h)),
        out_shape=jax.ShapeDtypeStruct((n_batch * seq_len, kv_heads * gw), BF16),
        scratch_shapes=[pltpu.VMEM((1, n_q), F32), pltpu.VMEM((1, n_q), F32), pltpu.VMEM((HEAD_DIM, n_q), F32)],
        compiler_params=_params(("arbitrary", "arbitrary", "arbitrary")),
    )(p, p, vt_ctx, p, vt_lat)


def _window_kernel(sink_ref, q_ref, k_ref, v_ref, kc_ref, vc_ref, o_ref, *, seq_len):
    kvh = pl.program_id(1)
    iq = pl.program_id(2)
    tq = q_ref.shape[0]
    win = tq + 2 * A_WINDOW
    q0 = iq * tq
    start = pl.multiple_of(jnp.clip(q0 - A_WINDOW, 0, seq_len - win), A_WINDOW)
    q = jnp.concatenate([q_ref[:, hh * HEAD_DIM:(hh + 1) * HEAD_DIM] for hh in range(A_GROUP)], axis=0)
    vw = v_ref[pl.ds(start, win), :]

    s_all = _dot_nt(q, k_ref[pl.ds(start, win), :])
    sc_all = _dot_nt(q, kc_ref[...])
    qpos = q0 + lax.broadcasted_iota(jnp.int32, (tq, win), 0)
    kpos = start + lax.broadcasted_iota(jnp.int32, (tq, win), 1)
    in_window = jnp.abs(kpos - qpos) <= A_WINDOW
    for hh in range(A_GROUP):
        rows = slice(hh * tq, (hh + 1) * tq)
        s = jnp.where(in_window, s_all[rows], NEG_INF)
        sc = sc_all[rows]
        sk = sink_ref[kvh * A_GROUP + hh]
        m = jnp.maximum(jnp.maximum(jnp.max(s, axis=-1, keepdims=True), jnp.max(sc, axis=-1, keepdims=True)), sk)
        p = jnp.exp2(s - m)
        pc = jnp.exp2(sc - m)
        l = jnp.sum(p, axis=-1, keepdims=True) + jnp.sum(pc, axis=-1, keepdims=True) + jnp.exp2(sk - m)
        o = (_dot(p.astype(BF16), vw) + _dot(pc.astype(BF16), vc_ref[...])) / l
        o_ref[:, hh * HEAD_DIM:(hh + 1) * HEAD_DIM] = o.astype(o_ref.dtype)


def _window_attention(p, sink, *, n_batch, seq_len, ctx_len, k_col0, v_col0, tq=128):
    nq = seq_len // tq
    gw = A_GROUP * HEAD_DIM
    kb, vb = k_col0 // HEAD_DIM, v_col0 // HEAD_DIM
    c0b = n_batch * seq_len // ctx_len
    return pl.pallas_call(
        functools.partial(_window_kernel, seq_len=seq_len),
        grid=(n_batch, A_KV_HEADS, nq),
        in_specs=[
            pl.BlockSpec(memory_space=pltpu.SMEM),
            pl.BlockSpec((tq, gw), lambda b, h, i: (b * nq + i, h)),
            pl.BlockSpec((seq_len, HEAD_DIM), lambda b, h, i: (b, kb + h)),
            pl.BlockSpec((seq_len, HEAD_DIM), lambda b, h, i: (b, vb + h)),
            pl.BlockSpec((ctx_len, HEAD_DIM), lambda b, h, i: (c0b + b, kb + h)),
            pl.BlockSpec((ctx_len, HEAD_DIM), lambda b, h, i: (c0b + b, vb + h)),
        ],
        out_specs=pl.BlockSpec((tq, gw), lambda b, h, i: (b * nq + i, h)),
        out_shape=jax.ShapeDtypeStruct((n_batch * seq_len, A_KV_HEADS * gw), BF16),
        compiler_params=_params(("arbitrary", "arbitrary", "arbitrary")),
    )(sink, p, p, p, p, p)


def _neighborhood_kernel(q_ref, k_ref, v_ref, kc_ref, vc_ref, bias_ref, o_ref, *, n_rows):
    n_keys = B_WIN_ROWS * GRID_W

    def body(t, carry):
        q0 = pl.multiple_of(t * (NA_ROWS * GRID_W), NA_ROWS * GRID_W)
        q_all = q_ref[pl.ds(q0, NA_ROWS * GRID_W), :]
        sc_all = _dot_nt(q_all, kc_ref[...])
        local = []
        for j in range(NA_ROWS):
            r = t * NA_ROWS + j
            rstart = jnp.clip(r - B_WIN_ROWS // 2, 0, n_rows - B_WIN_ROWS)
            k0 = pl.multiple_of(rstart * GRID_W, GRID_W)
            s = _dot_nt(q_all[j * GRID_W:(j + 1) * GRID_W], k_ref[pl.ds(k0, n_keys), :]) + bias_ref[0, r - rstart]
            local.append((k0, s))
        outs = []
        for j, (k0, s) in enumerate(local):
            sc = sc_all[j * GRID_W:(j + 1) * GRID_W]
            m = jnp.maximum(jnp.max(s, axis=-1, keepdims=True), jnp.max(sc, axis=-1, keepdims=True))
            p = jnp.exp2(s - m)
            pc = jnp.exp2(sc - m)
            l = jnp.sum(p, axis=-1, keepdims=True) + jnp.sum(pc, axis=-1, keepdims=True)
            o = (_dot(p.astype(BF16), v_ref[pl.ds(k0, n_keys), :]) + _dot(pc.astype(BF16), vc_ref[...])) / l
            outs.append(o.astype(o_ref.dtype))
        o_ref[pl.ds(q0, NA_ROWS * GRID_W), :] = jnp.concatenate(outs, axis=0)
        return carry

    lax.fori_loop(0, n_rows // NA_ROWS, body, 0)


def _neighborhood_bias(rpb):
    qc = np.arange(GRID_W)[:, None]
    kc = np.arange(GRID_W)[None, :]
    wstart = np.clip(qc - B_WIN_COLS // 2, 0, GRID_W - B_WIN_COLS)
    col_ok = (kc >= wstart) & (kc < wstart + B_WIN_COLS)
    dcol = np.clip(kc - qc + B_WIN_COLS - 1, 0, 2 * B_WIN_COLS - 2)
    d = np.arange(B_WIN_ROWS)[:, None]
    kk = np.arange(B_WIN_ROWS)[None, :]
    drow = kk - d + B_WIN_ROWS - 1
    bias = (rpb.astype(F32) * LOG2E)[:, drow][:, :, :, dcol]
    bias = jnp.where(col_ok[None, None, None], bias, NEG_INF)
    bias = jnp.transpose(bias, (0, 1, 3, 2, 4))
    return bias.reshape(rpb.shape[0], B_WIN_ROWS, GRID_W, B_WIN_ROWS * GRID_W)


def _neighborhood_attention(p, bias, *, n_batch, seq_len, ctx_len, q_col0, k_col0, v_col0):
    qb, kb, vb = q_col0 // HEAD_DIM, k_col0 // HEAD_DIM, v_col0 // HEAD_DIM
    c0b = n_batch * seq_len // ctx_len
    n_rows = seq_len // GRID_W
    return pl.pallas_call(
        functools.partial(_neighborhood_kernel, n_rows=n_rows),
        grid=(n_batch, B_HEADS),
        in_specs=[
            pl.BlockSpec((seq_len, HEAD_DIM), lambda b, h: (b, qb + h)),
            pl.BlockSpec((seq_len, HEAD_DIM), lambda b, h: (b, kb + h)),
            pl.BlockSpec((seq_len, HEAD_DIM), lambda b, h: (b, vb + h)),
            pl.BlockSpec((ctx_len, HEAD_DIM), lambda b, h: (c0b + b, kb + h)),
            pl.BlockSpec((ctx_len, HEAD_DIM), lambda b, h: (c0b + b, vb + h)),
            pl.BlockSpec((1, B_WIN_ROWS, GRID_W, B_WIN_ROWS * GRID_W), lambda b, h: (h, 0, 0, 0)),
        ],
        out_specs=pl.BlockSpec((seq_len, HEAD_DIM), lambda b, h: (b, h)),
        out_shape=jax.ShapeDtypeStruct((n_batch * seq_len, B_HEADS * HEAD_DIM), BF16),
        compiler_params=_params(("arbitrary", "arbitrary")),
    )(p, p, p, p, p, bias)


def _residual_ln_kernel(*refs, n_mm, n_lat_tiles):
    if n_mm:
        lat_refs = refs[:n_mm]
        ctx_refs = refs[n_mm:2 * n_mm]
        w_refs = refs[2 * n_mm:3 * n_mm]
        x_ref, g_ref, lg_ref, lb_ref, o_ref = refs[3 * n_mm:]
        is_lat = pl.program_id(0) < n_lat_tiles
        y = None
        for lat_ref, ctx_ref, w_ref in zip(lat_refs, ctx_refs, w_refs):
            a = jnp.where(is_lat, lat_ref[...], ctx_ref[...])
            yi = _dot(a, w_ref[...])
            y = yi if y is None else y + yi
    else:
        y_ref, x_ref, g_ref, lg_ref, lb_ref, o_ref = refs
        y = y_ref[...]
    z = DEEPNORM_ALPHA * x_ref[...] + g_ref[0] * y
    mu = jnp.mean(z, axis=-1, keepdims=True)
    zc = z - mu
    var = jnp.mean(zc * zc, axis=-1, keepdims=True)
    o_ref[...] = (zc * lax.rsqrt(var + 1e-5)) * lg_ref[...] + lb_ref[...]


def _residual_ln(x, gate, ln_g, ln_b, *, seq_len, n_lat_rows, mm=(), y=None, n_out_rows=None, tm=256):
    t, d = x.shape
    n_out_rows = n_out_rows or t
    n_batch = n_lat_rows // seq_len

    def mod_map(i):
        return (jnp.minimum((i * tm) // seq_len, n_batch), 0, 0)

    n_lat_tiles = n_lat_rows // tm
    n_ctx_tiles = (t - n_lat_rows) // tm
    in_specs, args = [], []
    for a_lat, _, _ in mm:
        in_specs.append(pl.BlockSpec((tm, a_lat.shape[1]), lambda i: (jnp.minimum(i, n_lat_tiles - 1), 0)))
        args.append(a_lat)
    for _, a_ctx, _ in mm:
        in_specs.append(pl.BlockSpec((tm, a_ctx.shape[1]),
                                     lambda i: (jnp.clip(i - n_lat_tiles, 0, n_ctx_tiles - 1), 0)))
        args.append(a_ctx)
    for _, _, w in mm:
        in_specs.append(pl.BlockSpec(w.shape, lambda i: (0, 0)))
        args.append(w)
    if not mm:
        in_specs.append(pl.BlockSpec((tm, d), lambda i: (i, 0)))
        args.append(y)
    in_specs += [
        pl.BlockSpec((tm, d), lambda i: (i, 0)),
        pl.BlockSpec((1, 1, d), mod_map),
        pl.BlockSpec((1, d), lambda i: (0, 0)),
        pl.BlockSpec((1, d), lambda i: (0, 0)),
    ]
    args += [x, gate, ln_g.reshape(1, d), ln_b.reshape(1, d)]
    return pl.pallas_call(
        functools.partial(_residual_ln_kernel, n_mm=len(mm), n_lat_tiles=n_lat_rows // tm),
        grid=(n_out_rows // tm,),
        in_specs=in_specs,
        out_specs=pl.BlockSpec((tm, d), lambda i: (i, 0)),
        out_shape=jax.ShapeDtypeStruct((n_out_rows, d), F32),
        compiler_params=_params(("arbitrary",)),
    )(*args)


def _top_values(xs, out_scrs):
    xs = list(xs)
    ridx = [lax.broadcasted_iota(jnp.int32, x.shape, 0).astype(F32) for x in xs]
    for k in range(PEER_TOPK):
        for n, out_scr in enumerate(out_scrs):
            x = xs[n]
            m = jnp.max(x, axis=0, keepdims=True)
            out_scr[k:k + 1, :] = m
            first = jnp.min(jnp.where(x == m, ridx[n], float(x.shape[0])), axis=0, keepdims=True)
            xs[n] = jnp.where(ridx[n] == first, -jnp.inf, x)


def _select_kernel(q_ref, sk_ref, s1_ref, e1_ref, s2_ref, e2_ref, thr_ref, t1_scr, t2_scr, cand_scr, best_scr):
    n_heads = sk_ref.shape[0]
    q = q_ref[...].astype(BF16)
    s1, s2 = [], []
    for h in range(n_heads):
        c0 = 2 * h * HEAD_DIM
        s1.append(_dot_nt(sk_ref[h, 0], q[:, c0:c0 + HEAD_DIM]))
        s2.append(_dot_nt(sk_ref[h, 1], q[:, c0 + HEAD_DIM:c0 + 2 * HEAD_DIM]))
    _top_values(s1 + s2, [t1_scr.at[h] for h in range(n_heads)] + [t2_scr.at[h] for h in range(n_heads)])
    sub = lax.broadcasted_iota(jnp.int32, (8, 1), 0)
    for h in range(n_heads):
        t1, t2, cand = t1_scr.at[h], t2_scr.at[h], cand_scr.at[h]
        cand[0:16, :] = t1[0:1, :] + t2[0:16, :]
        for a in range(1, 8):
            c = t1[a:a + 1, :] + t2[0:8, :]
            cand[8 + 8 * a:16 + 8 * a, :] = jnp.where(sub < PEER_TOPK // (a + 1), c, -jnp.inf)
        cand[72:80, :] = t1[8:16, :] + t2[0:1, :]
    _top_values([cand_scr[h] for h in range(n_heads)], [best_scr.at[h] for h in range(n_heads)])
    for h in range(n_heads):
        best = best_scr[h]
        z = jnp.sum(jnp.exp(best - best[0:1, :]), axis=0, keepdims=True)
        s1_ref[h] = s1[h]
        s2_ref[h] = s2[h]
        e1_ref[h] = jnp.exp(s1[h] - t1_scr[h, 0:1, :]) / z
        e2_ref[h] = jnp.exp(s2[h] - t2_scr[h, 0:1, :])
        thr_ref[h] = best[PEER_TOPK - 1:PEER_TOPK, :]


def _peer_select(q, subkeys, tt=256, heads_per_step=2):
    t = q.shape[0]
    nk = PEER_N_KEYS
    hps = heads_per_step
    big = jax.ShapeDtypeStruct((PEER_HEADS, nk, t), F32)
    big_spec = pl.BlockSpec((hps, nk, tt), lambda i, h: (h, 0, i))
    return pl.pallas_call(
        _select_kernel,
        grid=(t // tt, PEER_HEADS // hps),
        in_specs=[
            pl.BlockSpec((tt, hps * 2 * HEAD_DIM), lambda i, h: (i, h)),
            pl.BlockSpec((hps, 2, nk, HEAD_DIM), lambda i, h: (h, 0, 0, 0)),
        ],
        out_specs=[big_spec, big_spec, big_spec, big_spec, pl.BlockSpec((hps, 1, tt), lambda i, h: (h, 0, i))],
        out_shape=[big, big, big, big, jax.ShapeDtypeStruct((PEER_HEADS, 1, t), F32)],
        scratch_shapes=[pltpu.VMEM((hps, PEER_TOPK, tt), F32), pltpu.VMEM((hps, PEER_TOPK, tt), F32),
                        pltpu.VMEM((hps, 80, tt), F32), pltpu.VMEM((hps, PEER_TOPK, tt), F32)],
        compiler_params=_params(("arbitrary", "arbitrary")),
    )(q, subkeys)


def _peer_kernel(h_ref, u_ref, vt_ref, s1a_ref, e1a_ref, s1b_ref, e1b_ref, s2_ref, e2_ref, thr_ref, o_ref,
                 a0_scr, a1_scr, w0_scr, w1_scr, acc_scr):
    k = pl.program_id(1)
    te, tt = a0_scr.shape
    nk = PEER_N_KEYS
    ni = te // nk

    @pl.when(k == 0)
    def _():
        acc_scr[...] = jnp.zeros_like(acc_scr)
        a1_scr[...] = jnp.zeros_like(a1_scr)
        w0_scr[...] = jnp.zeros_like(w0_scr)
        w1_scr[...] = jnp.zeros_like(w1_scr)

    def gate_stage(a_scr, w_scr, s1_ref, e1_ref, row0):
        for tg in range(tt // 128):
            lanes = slice(tg * 128, (tg + 1) * 128)
            for jb in range(nk // GATE_ROWS):
                keys = slice(jb * GATE_ROWS, (jb + 1) * GATE_ROWS)
                gates = [None] * ni
                for hd in range(PEER_HEADS):
                    s2 = s2_ref[hd, keys, lanes]
                    e2 = e2_ref[hd, keys, lanes]
                    thr = thr_ref[hd, :, lanes]
                    for il in range(ni):
                        r = row0 + il
                        cand = s1_ref[hd, r:r + 1, lanes] + s2
                        sel = jnp.where(cand >= thr, e1_ref[hd, r:r + 1, lanes] * e2, 0.0)
                        gates[il] = sel if gates[il] is None else gates[il] + sel
                for il in range(ni):
                    rows = slice(il * nk + jb * GATE_ROWS, il * nk + (jb + 1) * GATE_ROWS)
                    a = a_scr[rows, lanes]
                    act = 0.5 * a * (1.0 + lax.erf(a * (2.0 ** -0.5)))
                    w_scr[rows, lanes] = (gates[il] * act).astype(BF16)

    def stage(u, vt, a_new, a_old, w_new, w_old, s1_ref, e1_ref, row0):
        a_new[...] = _dot_nt(u, h_ref[...])
        y = _dot(vt, w_old[...])
        gate_stage(a_old, w_new, s1_ref, e1_ref, row0)
        return y

    ya = stage(u_ref[:te, :], vt_ref[:, :te], a0_scr, a1_scr, w1_scr, w0_scr, s1a_ref, e1a_ref, ni)
    yb = stage(u_ref[te:, :], vt_ref[:, te:], a1_scr, a0_scr, w0_scr, w1_scr, s1b_ref, e1b_ref, 0)
    acc_scr[...] += ya + yb

    @pl.when(k == pl.num_programs(1) - 1)
    def _():
        o_ref[...] = acc_scr[...].T


PEER_TE = 512


def _peer_experts(h, u, vt, s1, e1, s2, e2, thr, *, tt=512):
    t, d = h.shape
    n_exp = u.shape[0]
    nk = PEER_N_KEYS
    te = PEER_TE
    nb = n_exp // (2 * te)
    last = nb - 1

    def row_spec(shift):
        return pl.BlockSpec((PEER_HEADS, 2 * te // nk, tt),
                            lambda i, k: (0, jnp.clip(k + shift, 0, last), i))

    col_spec = pl.BlockSpec((PEER_HEADS, nk, tt), lambda i, k: (0, 0, i))
    return pl.pallas_call(
        _peer_kernel,
        grid=(t // tt, nb + 1),
        in_specs=[
            pl.BlockSpec((tt, d), lambda i, k: (i, 0)),
            pl.BlockSpec((2 * te, d), lambda i, k: (jnp.minimum(k, last), 0)),
            pl.BlockSpec((d, 2 * te), lambda i, k: (0, jnp.maximum(k - 1, 0))),
            row_spec(-1), row_spec(-1), row_spec(0), row_spec(0), col_spec, col_spec,
            pl.BlockSpec((PEER_HEADS, 1, tt), lambda i, k: (0, 0, i)),
        ],
        out_specs=pl.BlockSpec((tt, d), lambda i, k: (i, 0)),
        out_shape=jax.ShapeDtypeStruct((t, d), F32),
        scratch_shapes=[pltpu.VMEM((te, tt), F32), pltpu.VMEM((te, tt), F32),
                        pltpu.VMEM((te, tt), BF16), pltpu.VMEM((te, tt), BF16), pltpu.VMEM((d, tt), F32)],
        compiler_params=_params(("arbitrary", "arbitrary")),
    )(h, u, vt, s1, e1, s1, e1, s2, e2, thr)


def _rope_tables(seq_len):
    t = jnp.arange(seq_len, dtype=jnp.int32)
    row = (t // GRID_W).astype(F32)
    col = (t % GRID_W).astype(F32)
    n_freq = HEAD_DIM // 4
    inv_freq = ROPE_THETA ** (-jnp.arange(n_freq, dtype=F32) / n_freq)
    ang_r = row[:, None] * inv_freq[None, :]
    ang_c = col[:, None] * inv_freq[None, :]
    ang = jnp.concatenate([ang_r, ang_r, ang_c, ang_c], axis=-1)
    sign = np.tile(np.repeat(np.array([-1.0, 1.0], np.float32), n_freq), 2)
    return jnp.cos(ang), jnp.sin(ang) * sign


def _col_flags(sizes_and_flags):
    out = []
    for size, fl in sizes_and_flags:
        out += [fl] * (size // FLAG_GROUP)
    return jnp.asarray(out, jnp.int32)


def kernel(x, c, ctx, c_ctx, mod_w, mod_b, ln_g, ln_b, ab_w_in, ab_w_out, a_sink, b_rpb, c_w_in, c_w_out,
           c_q_gain, c_k_gain, peer_wq, peer_subkeys, peer_u, peer_v):
    n_batch, seq_len, d = x.shape
    ctx_len = ctx.shape[1]
    n_layers = mod_w.shape[0]
    n_lat = n_batch * seq_len
    n_ctx = n_batch * ctx_len
    assert seq_len % ROW_TILE == 0 and n_ctx % ROW_TILE == 0 and seq_len % GRID_W == 0
    geo = dict(seq_len=seq_len, n_lat_rows=n_lat)

    xs = jnp.concatenate([x.reshape(n_lat, d), ctx.reshape(n_ctx, d)], axis=0)
    cos, sin_signed = _rope_tables(seq_len)

    c_rows = jnp.concatenate([c, c_ctx[None, :], jnp.zeros((8 - n_batch - 1, d), F32)], axis=0)
    mod = _modulation(c_rows, mod_w, mod_b).reshape(n_layers, 8, 6, d)

    def mod_vec(layer, k):
        return mod[layer, :n_batch + 1, k, :].reshape(n_batch + 1, 1, d)

    a_q, a_kv, b_w = A_Q_HEADS * HEAD_DIM, A_KV_HEADS * HEAD_DIM, B_HEADS * HEAD_DIM
    ab_flags = _col_flags([(a_q, FLAG_ROPE | FLAG_SCALE), (a_kv, FLAG_ROPE), (a_kv, 0),
                           (b_w, FLAG_SCALE), (b_w, 0), (b_w, 0)])
    c_q, c_kv = C_Q_HEADS * HEAD_DIM, C_KV_HEADS * HEAD_DIM
    c_flags = _col_flags([(c_q, FLAG_RMS_Q | FLAG_ROPE | FLAG_SCALE), (c_kv, FLAG_RMS_K | FLAG_ROPE), (c_kv, 0)])
    plain_flags = _col_flags([(peer_wq.shape[2], 0)])
    no_gain = jnp.zeros((8, HEAD_DIM), F32)

    for layer in range(n_layers):
        last = layer == n_layers - 1
        i = layer // 2
        sh1, sc1, g1, sh2, sc2, g2 = (mod_vec(layer, k) for k in range(6))

        if layer % 2 == 0:
            p = _projection(xs, sc1, sh1, ab_w_in[i].astype(BF16), ab_flags, cos, sin_signed, no_gain,
                            out_dtype=BF16, **geo)
            ka0, va0 = a_q, a_q + a_kv
            qb0 = a_q + 2 * a_kv
            kb0, vb0 = qb0 + b_w, qb0 + 2 * b_w
            sink = a_sink[i].astype(F32) * LOG2E
            out_a = _window_attention(p, sink, n_batch=n_batch, seq_len=seq_len, ctx_len=ctx_len,
                                      k_col0=ka0, v_col0=va0)
            out_b = _neighborhood_attention(p, _neighborhood_bias(b_rpb[i]), n_batch=n_batch, seq_len=seq_len,
                                            ctx_len=ctx_len, q_col0=qb0, k_col0=kb0, v_col0=vb0)
            ctx_a = _ctx_attention(p, sink, n_batch=n_batch, ctx_row0=n_lat, ctx_len=ctx_len, q_col0=0,
                                   kv_heads=A_KV_HEADS, group=A_GROUP, k_col0=ka0, v_col0=va0)
            ctx_b = _ctx_attention(p, None, n_batch=n_batch, ctx_row0=n_lat, ctx_len=ctx_len, q_col0=qb0,
                                   kv_heads=B_HEADS, group=1, k_col0=kb0, v_col0=vb0)
            w_out = ab_w_out[i].astype(BF16)
            mm = [(out_a, ctx_a, w_out[:a_q]), (out_b, ctx_b, w_out[a_q:])]
        else:
            gains = jnp.concatenate([c_q_gain[i][None], c_k_gain[i][None], jnp.zeros((6, HEAD_DIM), F32)], axis=0)
            p = _projection(xs, sc1, sh1, c_w_in[i].astype(BF16), c_flags, cos, sin_signed, gains,
                            out_dtype=BF16, **geo)
            kc0, vc0 = c_q, c_q + c_kv
            v_all = p[:, vc0:vc0 + c_kv]
            vt_lat = jnp.swapaxes(v_all[:n_lat].reshape(n_lat // DENSE_TK, DENSE_TK, c_kv), 1, 2)
            vt_ctx = jnp.swapaxes(v_all[n_lat:].reshape(n_batch, ctx_len, c_kv), 1, 2)
            out_c = _dense_attention(p, vt_lat, vt_ctx, n_batch=n_batch, seq_len=seq_len, ctx_len=ctx_len,
                                     kv_heads=C_KV_HEADS, group=C_GROUP, k_col0=kc0)
            ctx_c = _ctx_attention(p, None, n_batch=n_batch, ctx_row0=n_lat, ctx_len=ctx_len, q_col0=0,
                                   kv_heads=C_KV_HEADS, group=C_GROUP, k_col0=kc0, v_col0=vc0)
            mm = [(out_c, ctx_c, c_w_out[i].astype(BF16))]

        xs = _residual_ln(xs, g1, ln_g[layer, 0], ln_b[layer, 0], mm=mm, **geo)

        q, h = _projection(xs, sc2, sh2, peer_wq[layer].astype(BF16), plain_flags, cos, sin_signed, no_gain,
                           out_dtype=F32, emit_h=True, **geo)
        s1, e1, s2, e2, thr = _peer_select(q, peer_subkeys[layer].astype(BF16))
        y = _peer_experts(h, peer_u[layer].astype(BF16), peer_v[layer].astype(BF16).T, s1, e1, s2, e2, thr)
        xs = _residual_ln(xs, g2, ln_g[layer, 1], ln_b[layer, 1], y=y,
                          n_out_rows=n_lat if last else None, **geo)

    return xs.reshape(n_batch, seq_len, d)
```

```python
import functools

import numpy as np
import jax
import jax.numpy as jnp
from jax import lax
from jax.experimental import pallas as pl
from jax.experimental.pallas import tpu as pltpu

F32 = jnp.float32
BF16 = jnp.bfloat16

DEPTH = 4
GRID_W = 64
HEAD_DIM = 128
ATTN_SCALE = HEAD_DIM ** -0.5
LOG2E = 1.4426950408889634
Q_SCALE = ATTN_SCALE * LOG2E
ROPE_THETA = 10000.0
NEG_INF = -1e30

A_Q_HEADS = 8
A_KV_HEADS = 2
A_GROUP = A_Q_HEADS // A_KV_HEADS
A_WINDOW = 128
B_HEADS = 8
B_WIN_ROWS = 8
B_WIN_COLS = 16
NA_ROWS = 4
C_Q_HEADS = 16
C_KV_HEADS = 4
C_GROUP = C_Q_HEADS // C_KV_HEADS

PEER_HEADS = 8
PEER_N_KEYS = 128
PEER_TOPK = 16
GATE_ROWS = 128
DENSE_TK = 512
CHUNKS_PER_TRIP = 4

DEEPNORM_ALPHA = (2 * DEPTH) ** 0.25

VMEM_LIMIT_BYTES = 56 * 1024 * 1024

ROW_TILE = 512
COL_TILES = (1536, 1024, 512, 256)
FLAG_GROUP = 256


def _dot(a, b):
    return jnp.dot(a, b, preferred_element_type=F32)


def _dot_nt(a, b):
    return lax.dot_general(a, b, (((1,), (1,)), ((), ())), preferred_element_type=F32)


def _params(sem, vmem=None):
    return pltpu.CompilerParams(dimension_semantics=sem, vmem_limit_bytes=vmem or VMEM_LIMIT_BYTES)


def _mod_kernel(c_ref, w_ref, b_ref, o_ref):
    c = c_ref[...]
    a = c * (1.0 / (1.0 + jnp.exp(-c)))
    o_ref[0] = _dot(a.astype(BF16), w_ref[0].astype(BF16)) + b_ref[0]


def _modulation(c_rows, mod_w, mod_b):
    n_layers, d, n = mod_w.shape
    tn = 1024
    rows = c_rows.shape[0]
    return pl.pallas_call(
        _mod_kernel,
        grid=(n_layers, n // tn),
        in_specs=[
            pl.BlockSpec((rows, d), lambda l, j: (0, 0)),
            pl.BlockSpec((1, d, tn), lambda l, j: (l, 0, j)),
            pl.BlockSpec((1, 1, tn), lambda l, j: (l, 0, j)),
        ],
        out_specs=pl.BlockSpec((1, rows, tn), lambda l, j: (l, 0, j)),
        out_shape=jax.ShapeDtypeStruct((n_layers, rows, n), F32),
        compiler_params=_params(("arbitrary", "arbitrary")),
    )(c_rows, mod_w, mod_b.reshape(n_layers, 1, n))


FLAG_RMS_Q = 1
FLAG_RMS_K = 2
FLAG_ROPE = 4
FLAG_SCALE = 8


def _proj_kernel(flags_ref, x_ref, sc_ref, sh_ref, w_ref, cos_ref, sin_ref, gain_ref, o_ref, *rest,
                 n_lat_tiles, emit_h):
    if emit_h:
        h_out_ref, h_scr, acc_scr = rest
    else:
        h_scr, acc_scr = rest
    i = pl.program_id(0)
    j = pl.program_id(1)
    tn = o_ref.shape[1]

    @pl.when(j == 0)
    def _():
        h = (x_ref[...] * (1.0 + sc_ref[0]) + sh_ref[0]).astype(BF16)
        h_scr[...] = h
        if emit_h:
            h_out_ref[...] = h

    acc_scr[...] = _dot(h_scr[...], w_ref[...])
    is_lat = i < n_lat_tiles
    lane = lax.broadcasted_iota(jnp.int32, (1, HEAD_DIM), 1)
    odd_seg = ((lane // (HEAD_DIM // 4)) % 2) == 1

    for g in range(tn // FLAG_GROUP):
        fl = flags_ref[j * (tn // FLAG_GROUP) + g]
        rms = fl & 3
        do_rope = jnp.logical_and((fl & FLAG_ROPE) != 0, is_lat)
        scale = jnp.where((fl & FLAG_SCALE) != 0, Q_SCALE, 1.0).astype(F32)
        for hh in range(FLAG_GROUP // HEAD_DIM):
            c0 = g * FLAG_GROUP + hh * HEAD_DIM
            cols = slice(c0, c0 + HEAD_DIM)

            @pl.when(rms != 0)
            def _():
                y = acc_scr[:, cols]
                gain = gain_ref[pl.ds(rms - 1, 1), :]
                y = y * lax.rsqrt(jnp.mean(y * y, axis=-1, keepdims=True) + 1e-6)
                acc_scr[:, cols] = y * gain

            @pl.when(do_rope)
            def _():
                y = acc_scr[:, cols]
                r_dn = pltpu.roll(y, HEAD_DIM // 4, 1)
                r_up = pltpu.roll(y, 3 * HEAD_DIM // 4, 1)
                y = y * cos_ref[...] + jnp.where(odd_seg, r_dn, r_up) * sin_ref[...]
                o_ref[:, cols] = (y * scale).astype(o_ref.dtype)

            @pl.when(jnp.logical_not(do_rope))
            def _():
                o_ref[:, cols] = (acc_scr[:, cols] * scale).astype(o_ref.dtype)


def _projection(x, sc, sh, w, flags, cos, sin_signed, gains, *, seq_len, n_lat_rows, out_dtype, emit_h=False):
    t, d = x.shape
    n = w.shape[1]
    tm = ROW_TILE
    tn = next(c for c in COL_TILES if n % c == 0)
    n_lat_tiles = n_lat_rows // tm
    pos_tiles = seq_len // tm
    n_batch = n_lat_rows // seq_len

    def mod_map(i, j, fl):
        return (jnp.minimum((i * tm) // seq_len, n_batch), 0, 0)

    out_shape = [jax.ShapeDtypeStruct((t, n), out_dtype)]
    out_specs = [pl.BlockSpec((tm, tn), lambda i, j, fl: (i, j))]
    if emit_h:
        out_shape.append(jax.ShapeDtypeStruct((t, d), BF16))
        out_specs.append(pl.BlockSpec((tm, d), lambda i, j, fl: (i, 0)))
    grid_spec = pltpu.PrefetchScalarGridSpec(
        num_scalar_prefetch=1,
        grid=(t // tm, n // tn),
        in_specs=[
            pl.BlockSpec((tm, d), lambda i, j, fl: (i, 0)),
            pl.BlockSpec((1, 1, d), mod_map),
            pl.BlockSpec((1, 1, d), mod_map),
            pl.BlockSpec((d, tn), lambda i, j, fl: (0, j)),
            pl.BlockSpec((tm, HEAD_DIM), lambda i, j, fl: (i % pos_tiles, 0)),
            pl.BlockSpec((tm, HEAD_DIM), lambda i, j, fl: (i % pos_tiles, 0)),
            pl.BlockSpec((8, HEAD_DIM), lambda i, j, fl: (0, 0)),
        ],
        out_specs=out_specs,
        scratch_shapes=[pltpu.VMEM((tm, d), BF16), pltpu.VMEM((tm, tn), F32)],
    )
    res = pl.pallas_call(
        functools.partial(_proj_kernel, n_lat_tiles=n_lat_tiles, emit_h=emit_h),
        grid_spec=grid_spec,
        out_shape=out_shape,
        compiler_params=_params(("arbitrary", "arbitrary")),
    )(flags, x, sc, sh, w, cos, sin_signed, gains)
    return res if emit_h else res[0]


def _ctx_attn_kernel(sink_ref, q_ref, k_ref, v_ref, o_ref, *, group, has_sink):
    tq = q_ref.shape[0]
    q = jnp.concatenate([q_ref[:, hh * HEAD_DIM:(hh + 1) * HEAD_DIM] for hh in range(group)], axis=0)
    s = _dot_nt(q, k_ref[...])
    m = jnp.max(s, axis=-1, keepdims=True)
    if has_sink:
        kvh = pl.program_id(1)
        sk = jnp.concatenate(
            [jnp.full((tq, 1), sink_ref[kvh * group + hh], F32) for hh in range(group)], axis=0)
        m = jnp.maximum(m, sk)
    p = jnp.exp2(s - m)
    l = jnp.sum(p, axis=-1, keepdims=True)
    if has_sink:
        l = l + jnp.exp2(sk - m)
    o = _dot(p.astype(BF16), v_ref[...]) / l
    for hh in range(group):
        o_ref[:, hh * HEAD_DIM:(hh + 1) * HEAD_DIM] = o[hh * tq:(hh + 1) * tq].astype(o_ref.dtype)


def _ctx_attention(p, sink, *, n_batch, ctx_row0, ctx_len, q_col0, kv_heads, group, k_col0, v_col0):
    has_sink = sink is not None
    if sink is None:
        sink = jnp.zeros((kv_heads * group,), F32)
    gw = group * HEAD_DIM
    kb, vb, qb = k_col0 // HEAD_DIM, v_col0 // HEAD_DIM, q_col0 // gw
    c0b = ctx_row0 // ctx_len
    return pl.pallas_call(
        functools.partial(_ctx_attn_kernel, group=group, has_sink=has_sink),
        grid=(n_batch, kv_heads),
        in_specs=[
            pl.BlockSpec(memory_space=pltpu.SMEM),
            pl.BlockSpec((ctx_len, gw), lambda b, h: (c0b + b, qb + h)),
            pl.BlockSpec((ctx_len, HEAD_DIM), lambda b, h: (c0b + b, kb + h)),
            pl.BlockSpec((ctx_len, HEAD_DIM), lambda b, h: (c0b + b, vb + h)),
        ],
        out_specs=pl.BlockSpec((ctx_len, gw), lambda b, h: (b, h)),
        out_shape=jax.ShapeDtypeStruct((n_batch * ctx_len, kv_heads * gw), BF16),
        compiler_params=_params(("arbitrary", "arbitrary")),
    )(sink, p, p, p)


def _dense_attn_kernel(q_ref, kc_ref, vtc_ref, kl_ref, vtl_ref, o_ref, m_scr, l_scr, acc_scr, *, group, lanes,
                       per_trip):
    tq = q_ref.shape[0]
    n_q = group * tq
    n_chunks = vtl_ref.shape[0]
    tk = vtl_ref.shape[2]
    q = jnp.concatenate([q_ref[:, hh * HEAD_DIM:(hh + 1) * HEAD_DIM] for hh in range(group)], axis=0)
    q_groups = [q[g * lanes:(g + 1) * lanes] for g in range(n_q // lanes)]

    def softmax_step(state, s, vt):
        mx = jnp.max(s, axis=0, keepdims=True)
        if state is None:
            m_new = mx
        else:
            m_old, l_old, acc_old = state
            m_new = jnp.maximum(m_old, mx)
            a = jnp.exp2(m_old - m_new)
        p = jnp.exp2(s - m_new)
        psum = jnp.sum(p, axis=0, keepdims=True)
        pv = _dot(vt, p.astype(BF16))
        if state is None:
            return m_new, psum, pv
        return m_new, a * l_old + psum, a * acc_old + pv

    def update(chunks, first):
        scores = [[_dot_nt(k, qg) for qg in q_groups] for k, _ in chunks]
        states = []
        for g in range(len(q_groups)):
            cols = slice(g * lanes, (g + 1) * lanes)
            states.append(None if first else (m_scr[:, cols], l_scr[:, cols], acc_scr[:, cols]))
        for ci, (_, vt) in enumerate(chunks):
            states = [softmax_step(states[g], scores[ci][g], vt) for g in range(len(q_groups))]
        for g, (m, l, acc) in enumerate(states):
            cols = slice(g * lanes, (g + 1) * lanes)
            m_scr[:, cols] = m
            l_scr[:, cols] = l
            acc_scr[:, cols] = acc

    update([(kc_ref[...], vtc_ref[0])], True)

    def body(c, carry):
        chunks = []
        for half in range(per_trip):
            cc = per_trip * c + half
            r0 = pl.multiple_of(cc * tk, tk)
            chunks.append((kl_ref[pl.ds(r0, tk), :], vtl_ref[cc]))
        update(chunks, False)
        return carry

    lax.fori_loop(0, n_chunks // per_trip, body, 0)

    o = (acc_scr[...] / l_scr[...]).T
    for hh in range(group):
        o_ref[:, hh * HEAD_DIM:(hh + 1) * HEAD_DIM] = o[hh * tq:(hh + 1) * tq].astype(o_ref.dtype)


def _dense_attention(p, vt_lat, vt_ctx, *, n_batch, seq_len, ctx_len, kv_heads, group, k_col0, tq=512, lanes=256):
    nq = seq_len // tq
    gw = group * HEAD_DIM
    kb = k_col0 // HEAD_DIM
    c0b = n_batch * seq_len // ctx_len
    n_chunks = vt_lat.shape[0] // n_batch
    tk = vt_lat.shape[2]
    n_q = group * tq
    return pl.pallas_call(
        functools.partial(_dense_attn_kernel, group=group, lanes=lanes,
                          per_trip=next(c for c in (CHUNKS_PER_TRIP, 2, 1) if n_chunks % c == 0)),
        grid=(n_batch, kv_heads, nq),
        in_specs=[
            pl.BlockSpec((tq, gw), lambda b, h, i: (b * nq + i, h)),
            pl.BlockSpec((ctx_len, HEAD_DIM), lambda b, h, i: (c0b + b, kb + h)),
            pl.BlockSpec((1, HEAD_DIM, ctx_len), lambda b, h, i: (b, h, 0)),
            pl.BlockSpec((seq_len, HEAD_DIM), lambda b, h, i: (b, kb + h)),
            pl.BlockSpec((n_chunks, HEAD_DIM, tk), lambda b, h, i: (b, h, 0)),
        ],
        out_specs=pl.BlockSpec((tq, gw), lambda b, h, i: (b * nq + i, h)),
        out_shape=jax.ShapeDtypeStruct((n_batch * seq_len, kv_heads * gw), BF16),
        scratch_shapes=[pltpu.VMEM((1, n_q), F32), pltpu.VMEM((1, n_q), F32), pltpu.VMEM((HEAD_DIM, n_q), F32)],
        compiler_params=_params(("arbitrary", "arbitrary", "arbitrary")),
    )(p, p, vt_ctx, p, vt_lat)


def _window_kernel(sink_ref, q_ref, k_ref, v_ref, kc_ref, vc_ref, o_ref, *, seq_len):
    kvh = pl.program_id(1)
    iq = pl.program_id(2)
    tq = q_ref.shape[0]
    win = tq + 2 * A_WINDOW
    q0 = iq * tq
    start = pl.multiple_of(jnp.clip(q0 - A_WINDOW, 0, seq_len - win), A_WINDOW)
    q = jnp.concatenate([q_ref[:, hh * HEAD_DIM:(hh + 1) * HEAD_DIM] for hh in range(A_GROUP)], axis=0)
    vw = v_ref[pl.ds(start, win), :]

    s_all = _dot_nt(q, k_ref[pl.ds(start, win), :])
    sc_all = _dot_nt(q, kc_ref[...])
    qpos = q0 + lax.broadcasted_iota(jnp.int32, (tq, win), 0)
    kpos = start + lax.broadcasted_iota(jnp.int32, (tq, win), 1)
    in_window = jnp.abs(kpos - qpos) <= A_WINDOW
    for hh in range(A_GROUP):
        rows = slice(hh * tq, (hh + 1) * tq)
        s = jnp.where(in_window, s_all[rows], NEG_INF)
        sc = sc_all[rows]
        sk = sink_ref[kvh * A_GROUP + hh]
        m = jnp.maximum(jnp.maximum(jnp.max(s, axis=-1, keepdims=True), jnp.max(sc, axis=-1, keepdims=True)), sk)
        p = jnp.exp2(s - m)
        pc = jnp.exp2(sc - m)
        l = jnp.sum(p, axis=-1, keepdims=True) + jnp.sum(pc, axis=-1, keepdims=True) + jnp.exp2(sk - m)
        o = (_dot(p.astype(BF16), vw) + _dot(pc.astype(BF16), vc_ref[...])) / l
        o_ref[:, hh * HEAD_DIM:(hh + 1) * HEAD_DIM] = o.astype(o_ref.dtype)


def _window_attention(p, sink, *, n_batch, seq_len, ctx_len, k_col0, v_col0, tq=128):
    nq = seq_len // tq
    gw = A_GROUP * HEAD_DIM
    kb, vb = k_col0 // HEAD_DIM, v_col0 // HEAD_DIM
    c0b = n_batch * seq_len // ctx_len
    return pl.pallas_call(
        functools.partial(_window_kernel, seq_len=seq_len),
        grid=(n_batch, A_KV_HEADS, nq),
        in_specs=[
            pl.BlockSpec(memory_space=pltpu.SMEM),
            pl.BlockSpec((tq, gw), lambda b, h, i: (b * nq + i, h)),
            pl.BlockSpec((seq_len, HEAD_DIM), lambda b, h, i: (b, kb + h)),
            pl.BlockSpec((seq_len, HEAD_DIM), lambda b, h, i: (b, vb + h)),
            pl.BlockSpec((ctx_len, HEAD_DIM), lambda b, h, i: (c0b + b, kb + h)),
            pl.BlockSpec((ctx_len, HEAD_DIM), lambda b, h, i: (c0b + b, vb + h)),
        ],
        out_specs=pl.BlockSpec((tq, gw), lambda b, h, i: (b * nq + i, h)),
        out_shape=jax.ShapeDtypeStruct((n_batch * seq_len, A_KV_HEADS * gw), BF16),
        compiler_params=_params(("arbitrary", "arbitrary", "arbitrary")),
    )(sink, p, p, p, p, p)


def _neighborhood_kernel(q_ref, k_ref, v_ref, kc_ref, vc_ref, bias_ref, o_ref, *, n_rows):
    n_keys = B_WIN_ROWS * GRID_W

    def body(t, carry):
        q0 = pl.multiple_of(t * (NA_ROWS * GRID_W), NA_ROWS * GRID_W)
        q_all = q_ref[pl.ds(q0, NA_ROWS * GRID_W), :]
        sc_all = _dot_nt(q_all, kc_ref[...])
        local = []
        for j in range(NA_ROWS):
            r = t * NA_ROWS + j
            rstart = jnp.clip(r - B_WIN_ROWS // 2, 0, n_rows - B_WIN_ROWS)
            k0 = pl.multiple_of(rstart * GRID_W, GRID_W)
            s = _dot_nt(q_all[j * GRID_W:(j + 1) * GRID_W], k_ref[pl.ds(k0, n_keys), :]) + bias_ref[0, r - rstart]
            local.append((k0, s))
        outs = []
        for j, (k0, s) in enumerate(local):
            sc = sc_all[j * GRID_W:(j + 1) * GRID_W]
            m = jnp.maximum(jnp.max(s, axis=-1, keepdims=True), jnp.max(sc, axis=-1, keepdims=True))
            p = jnp.exp2(s - m)
            pc = jnp.exp2(sc - m)
            l = jnp.sum(p, axis=-1, keepdims=True) + jnp.sum(pc, axis=-1, keepdims=True)
            o = (_dot(p.astype(BF16), v_ref[pl.ds(k0, n_keys), :]) + _dot(pc.astype(BF16), vc_ref[...])) / l
            outs.append(o.astype(o_ref.dtype))
        o_ref[pl.ds(q0, NA_ROWS * GRID_W), :] = jnp.concatenate(outs, axis=0)
        return carry

    lax.fori_loop(0, n_rows // NA_ROWS, body, 0)


def _neighborhood_bias(rpb):
    qc = np.arange(GRID_W)[:, None]
    kc = np.arange(GRID_W)[None, :]
    wstart = np.clip(qc - B_WIN_COLS // 2, 0, GRID_W - B_WIN_COLS)
    col_ok = (kc >= wstart) & (kc < wstart + B_WIN_COLS)
    dcol = np.clip(kc - qc + B_WIN_COLS - 1, 0, 2 * B_WIN_COLS - 2)
    d = np.arange(B_WIN_ROWS)[:, None]
    kk = np.arange(B_WIN_ROWS)[None, :]
    drow = kk - d + B_WIN_ROWS - 1
    bias = (rpb.astype(F32) * LOG2E)[:, drow][:, :, :, dcol]
    bias = jnp.where(col_ok[None, None, None], bias, NEG_INF)
    bias = jnp.transpose(bias, (0, 1, 3, 2, 4))
    return bias.reshape(rpb.shape[0], B_WIN_ROWS, GRID_W, B_WIN_ROWS * GRID_W)


def _neighborhood_attention(p, bias, *, n_batch, seq_len, ctx_len, q_col0, k_col0, v_col0):
    qb, kb, vb = q_col0 // HEAD_DIM, k_col0 // HEAD_DIM, v_col0 // HEAD_DIM
    c0b = n_batch * seq_len // ctx_len
    n_rows = seq_len // GRID_W
    return pl.pallas_call(
        functools.partial(_neighborhood_kernel, n_rows=n_rows),
        grid=(n_batch, B_HEADS),
        in_specs=[
            pl.BlockSpec((seq_len, HEAD_DIM), lambda b, h: (b, qb + h)),
            pl.BlockSpec((seq_len, HEAD_DIM), lambda b, h: (b, kb + h)),
            pl.BlockSpec((seq_len, HEAD_DIM), lambda b, h: (b, vb + h)),
            pl.BlockSpec((ctx_len, HEAD_DIM), lambda b, h: (c0b + b, kb + h)),
            pl.BlockSpec((ctx_len, HEAD_DIM), lambda b, h: (c0b + b, vb + h)),
            pl.BlockSpec((1, B_WIN_ROWS, GRID_W, B_WIN_ROWS * GRID_W), lambda b, h: (h, 0, 0, 0)),
        ],
        out_specs=pl.BlockSpec((seq_len, HEAD_DIM), lambda b, h: (b, h)),
        out_shape=jax.ShapeDtypeStruct((n_batch * seq_len, B_HEADS * HEAD_DIM), BF16),
        compiler_params=_params(("arbitrary", "arbitrary")),
    )(p, p, p, p, p, bias)


def _residual_ln_kernel(*refs, n_mm, n_lat_tiles):
    if n_mm:
        lat_refs = refs[:n_mm]
        ctx_refs = refs[n_mm:2 * n_mm]
        w_refs = refs[2 * n_mm:3 * n_mm]
        x_ref, g_ref, lg_ref, lb_ref, o_ref = refs[3 * n_mm:]
        is_lat = pl.program_id(0) < n_lat_tiles
        y = None
        for lat_ref, ctx_ref, w_ref in zip(lat_refs, ctx_refs, w_refs):
            a = jnp.where(is_lat, lat_ref[...], ctx_ref[...])
            yi = _dot(a, w_ref[...])
            y = yi if y is None else y + yi
    else:
        y_ref, x_ref, g_ref, lg_ref, lb_ref, o_ref = refs
        y = y_ref[...]
    z = DEEPNORM_ALPHA * x_ref[...] + g_ref[0] * y
    mu = jnp.mean(z, axis=-1, keepdims=True)
    zc = z - mu
    var = jnp.mean(zc * zc, axis=-1, keepdims=True)
    o_ref[...] = (zc * lax.rsqrt(var + 1e-5)) * lg_ref[...] + lb_ref[...]


def _residual_ln(x, gate, ln_g, ln_b, *, seq_len, n_lat_rows, mm=(), y=None, n_out_rows=None, tm=256):
    t, d = x.shape
    n_out_rows = n_out_rows or t
    n_batch = n_lat_rows // seq_len

    def mod_map(i):
        return (jnp.minimum((i * tm) // seq_len, n_batch), 0, 0)

    n_lat_tiles = n_lat_rows // tm
    n_ctx_tiles = (t - n_lat_rows) // tm
    in_specs, args = [], []
    for a_lat, _, _ in mm:
        in_specs.append(pl.BlockSpec((tm, a_lat.shape[1]), lambda i: (jnp.minimum(i, n_lat_tiles - 1), 0)))
        args.append(a_lat)
    for _, a_ctx, _ in mm:
        in_specs.append(pl.BlockSpec((tm, a_ctx.shape[1]),
                                     lambda i: (jnp.clip(i - n_lat_tiles, 0, n_ctx_tiles - 1), 0)))
        args.append(a_ctx)
    for _, _, w in mm:
        in_specs.append(pl.BlockSpec(w.shape, lambda i: (0, 0)))
        args.append(w)
    if not mm:
        in_specs.append(pl.BlockSpec((tm, d), lambda i: (i, 0)))
        args.append(y)
    in_specs += [
        pl.BlockSpec((tm, d), lambda i: (i, 0)),
        pl.BlockSpec((1, 1, d), mod_map),
        pl.BlockSpec((1, d), lambda i: (0, 0)),
        pl.BlockSpec((1, d), lambda i: (0, 0)),
    ]
    args += [x, gate, ln_g.reshape(1, d), ln_b.reshape(1, d)]
    return pl.pallas_call(
        functools.partial(_residual_ln_kernel, n_mm=len(mm), n_lat_tiles=n_lat_rows // tm),
        grid=(n_out_rows // tm,),
        in_specs=in_specs,
        out_specs=pl.BlockSpec((tm, d), lambda i: (i, 0)),
        out_shape=jax.ShapeDtypeStruct((n_out_rows, d), F32),
        compiler_params=_params(("arbitrary",)),
    )(*args)


def _top_values(xs, out_scrs):
    xs = list(xs)
    ridx = [lax.broadcasted_iota(jnp.int32, x.shape, 0).astype(F32) for x in xs]
    for k in range(PEER_TOPK):
        for n, out_scr in enumerate(out_scrs):
            x = xs[n]
            m = jnp.max(x, axis=0, keepdims=True)
            out_scr[k:k + 1, :] = m
            first = jnp.min(jnp.where(x == m, ridx[n], float(x.shape[0])), axis=0, keepdims=True)
            xs[n] = jnp.where(ridx[n] == first, -jnp.inf, x)


def _select_kernel(q_ref, sk_ref, s1_ref, e1_ref, s2_ref, e2_ref, thr_ref, t1_scr, t2_scr, cand_scr, best_scr):
    n_heads = sk_ref.shape[0]
    q = q_ref[...].astype(BF16)
    s1, s2 = [], []
    for h in range(n_heads):
        c0 = 2 * h * HEAD_DIM
        s1.append(_dot_nt(sk_ref[h, 0], q[:, c0:c0 + HEAD_DIM]))
        s2.append(_dot_nt(sk_ref[h, 1], q[:, c0 + HEAD_DIM:c0 + 2 * HEAD_DIM]))
    _top_values(s1 + s2, [t1_scr.at[h] for h in range(n_heads)] + [t2_scr.at[h] for h in range(n_heads)])
    sub = lax.broadcasted_iota(jnp.int32, (8, 1), 0)
    for h in range(n_heads):
        t1, t2, cand = t1_scr.at[h], t2_scr.at[h], cand_scr.at[h]
        cand[0:16, :] = t1[0:1, :] + t2[0:16, :]
        for a in range(1, 8):
            c = t1[a:a + 1, :] + t2[0:8, :]
            cand[8 + 8 * a:16 + 8 * a, :] = jnp.where(sub < PEER_TOPK // (a + 1), c, -jnp.inf)
        cand[72:80, :] = t1[8:16, :] + t2[0:1, :]
    _top_values([cand_scr[h] for h in range(n_heads)], [best_scr.at[h] for h in range(n_heads)])
    for h in range(n_heads):
        best = best_scr[h]
        z = jnp.sum(jnp.exp(best - best[0:1, :]), axis=0, keepdims=True)
        s1_ref[h] = s1[h]
        s2_ref[h] = s2[h]
        e1_ref[h] = jnp.exp(s1[h] - t1_scr[h, 0:1, :]) / z
        e2_ref[h] = jnp.exp(s2[h] - t2_scr[h, 0:1, :])
        thr_ref[h] = best[PEER_TOPK - 1:PEER_TOPK, :]


def _peer_select(q, subkeys, tt=256, heads_per_step=2):
    t = q.shape[0]
    nk = PEER_N_KEYS
    hps = heads_per_step
    big = jax.ShapeDtypeStruct((PEER_HEADS, nk, t), F32)
    big_spec = pl.BlockSpec((hps, nk, tt), lambda i, h: (h, 0, i))
    return pl.pallas_call(
        _select_kernel,
        grid=(t // tt, PEER_HEADS // hps),
        in_specs=[
            pl.BlockSpec((tt, hps * 2 * HEAD_DIM), lambda i, h: (i, h)),
            pl.BlockSpec((hps, 2, nk, HEAD_DIM), lambda i, h: (h, 0, 0, 0)),
        ],
        out_specs=[big_spec, big_spec, big_spec, big_spec, pl.BlockSpec((hps, 1, tt), lambda i, h: (h, 0, i))],
        out_shape=[big, big, big, big, jax.ShapeDtypeStruct((PEER_HEADS, 1, t), F32)],
        scratch_shapes=[pltpu.VMEM((hps, PEER_TOPK, tt), F32), pltpu.VMEM((hps, PEER_TOPK, tt), F32),
                        pltpu.VMEM((hps, 80, tt), F32), pltpu.VMEM((hps, PEER_TOPK, tt), F32)],
        compiler_params=_params(("arbitrary", "arbitrary")),
    )(q, subkeys)


def _peer_kernel(h_ref, u_ref, vt_ref, s1a_ref, e1a_ref, s1b_ref, e1b_ref, s2_ref, e2_ref, thr_ref, o_ref,
                 a0_scr, a1_scr, w0_scr, w1_scr, acc_scr):
    k = pl.program_id(1)
    te, tt = a0_scr.shape
    nk = PEER_N_KEYS
    ni = te // nk

    @pl.when(k == 0)
    def _():
        acc_scr[...] = jnp.zeros_like(acc_scr)
        a1_scr[...] = jnp.zeros_like(a1_scr)
        w0_scr[...] = jnp.zeros_like(w0_scr)
        w1_scr[...] = jnp.zeros_like(w1_scr)

    def gate_stage(a_scr, w_scr, s1_ref, e1_ref, row0):
        for tg in range(tt // 128):
            lanes = slice(tg * 128, (tg + 1) * 128)
            for jb in range(nk // GATE_ROWS):
                keys = slice(jb * GATE_ROWS, (jb + 1) * GATE_ROWS)
                gates = [None] * ni
                for hd in range(PEER_HEADS):
                    s2 = s2_ref[hd, keys, lanes]
                    e2 = e2_ref[hd, keys, lanes]
                    thr = thr_ref[hd, :, lanes]
                    for il in range(ni):
                        r = row0 + il
                        cand = s1_ref[hd, r:r + 1, lanes] + s2
                        sel = jnp.where(cand >= thr, e1_ref[hd, r:r + 1, lanes] * e2, 0.0)
                        gates[il] = sel if gates[il] is None else gates[il] + sel
                for il in range(ni):
                    rows = slice(il * nk + jb * GATE_ROWS, il * nk + (jb + 1) * GATE_ROWS)
                    a = a_scr[rows, lanes]
                    act = 0.5 * a * (1.0 + lax.erf(a * (2.0 ** -0.5)))
                    w_scr[rows, lanes] = (gates[il] * act).astype(BF16)

    def stage(u, vt, a_new, a_old, w_new, w_old, s1_ref, e1_ref, row0):
        a_new[...] = _dot_nt(u, h_ref[...])
        y = _dot(vt, w_old[...])
        gate_stage(a_old, w_new, s1_ref, e1_ref, row0)
        return y

    ya = stage(u_ref[:te, :], vt_ref[:, :te], a0_scr, a1_scr, w1_scr, w0_scr, s1a_ref, e1a_ref, ni)
    yb = stage(u_ref[te:, :], vt_ref[:, te:], a1_scr, a0_scr, w0_scr, w1_scr, s1b_ref, e1b_ref, 0)
    acc_scr[...] += ya + yb

    @pl.when(k == pl.num_programs(1) - 1)
    def _():
        o_ref[...] = acc_scr[...].T


PEER_TE = 512


def _peer_experts(h, u, vt, s1, e1, s2, e2, thr, *, tt=512):
    t, d = h.shape
    n_exp = u.shape[0]
    nk = PEER_N_KEYS
    te = PEER_TE
    nb = n_exp // (2 * te)
    last = nb - 1

    def row_spec(shift):
        return pl.BlockSpec((PEER_HEADS, 2 * te // nk, tt),
                            lambda i, k: (0, jnp.clip(k + shift, 0, last), i))

    col_spec = pl.BlockSpec((PEER_HEADS, nk, tt), lambda i, k: (0, 0, i))
    return pl.pallas_call(
        _peer_kernel,
        grid=(t // tt, nb + 1),
        in_specs=[
            pl.BlockSpec((tt, d), lambda i, k: (i, 0)),
            pl.BlockSpec((2 * te, d), lambda i, k: (jnp.minimum(k, last), 0)),
            pl.BlockSpec((d, 2 * te), lambda i, k: (0, jnp.maximum(k - 1, 0))),
            row_spec(-1), row_spec(-1), row_spec(0), row_spec(0), col_spec, col_spec,
            pl.BlockSpec((PEER_HEADS, 1, tt), lambda i, k: (0, 0, i)),
        ],
        out_specs=pl.BlockSpec((tt, d), lambda i, k: (i, 0)),
        out_shape=jax.ShapeDtypeStruct((t, d), F32),
        scratch_shapes=[pltpu.VMEM((te, tt), F32), pltpu.VMEM((te, tt), F32),
                        pltpu.VMEM((te, tt), BF16), pltpu.VMEM((te, tt), BF16), pltpu.VMEM((d, tt), F32)],
        compiler_params=_params(("arbitrary", "arbitrary")),
    )(h, u, vt, s1, e1, s1, e1, s2, e2, thr)


def _rope_tables(seq_len):
    t = jnp.arange(seq_len, dtype=jnp.int32)
    row = (t // GRID_W).astype(F32)
    col = (t % GRID_W).astype(F32)
    n_freq = HEAD_DIM // 4
    inv_freq = ROPE_THETA ** (-jnp.arange(n_freq, dtype=F32) / n_freq)
    ang_r = row[:, None] * inv_freq[None, :]
    ang_c = col[:, None] * inv_freq[None, :]
    ang = jnp.concatenate([ang_r, ang_r, ang_c, ang_c], axis=-1)
    sign = np.tile(np.repeat(np.array([-1.0, 1.0], np.float32), n_freq), 2)
    return jnp.cos(ang), jnp.sin(ang) * sign


def _col_flags(sizes_and_flags):
    out = []
    for size, fl in sizes_and_flags:
        out += [fl] * (size // FLAG_GROUP)
    return jnp.asarray(out, jnp.int32)


def kernel(x, c, ctx, c_ctx, mod_w, mod_b, ln_g, ln_b, ab_w_in, ab_w_out, a_sink, b_rpb, c_w_in, c_w_out,
           c_q_gain, c_k_gain, peer_wq, peer_subkeys, peer_u, peer_v):
    n_batch, seq_len, d = x.shape
    ctx_len = ctx.shape[1]
    n_layers = mod_w.shape[0]
    n_lat = n_batch * seq_len
    n_ctx = n_batch * ctx_len
    assert seq_len % ROW_TILE == 0 and n_ctx % ROW_TILE == 0 and seq_len % GRID_W == 0
    geo = dict(seq_len=seq_len, n_lat_rows=n_lat)

    xs = jnp.concatenate([x.reshape(n_lat, d), ctx.reshape(n_ctx, d)], axis=0)
    cos, sin_signed = _rope_tables(seq_len)

    c_rows = jnp.concatenate([c, c_ctx[None, :], jnp.zeros((8 - n_batch - 1, d), F32)], axis=0)
    mod = _modulation(c_rows, mod_w, mod_b).reshape(n_layers, 8, 6, d)

    def mod_vec(layer, k):
        return mod[layer, :n_batch + 1, k, :].reshape(n_batch + 1, 1, d)

    a_q, a_kv, b_w = A_Q_HEADS * HEAD_DIM, A_KV_HEADS * HEAD_DIM, B_HEADS * HEAD_DIM
    ab_flags = _col_flags([(a_q, FLAG_ROPE | FLAG_SCALE), (a_kv, FLAG_ROPE), (a_kv, 0),
                           (b_w, FLAG_SCALE), (b_w, 0), (b_w, 0)])
    c_q, c_kv = C_Q_HEADS * HEAD_DIM, C_KV_HEADS * HEAD_DIM
    c_flags = _col_flags([(c_q, FLAG_RMS_Q | FLAG_ROPE | FLAG_SCALE), (c_kv, FLAG_RMS_K | FLAG_ROPE), (c_kv, 0)])
    plain_flags = _col_flags([(peer_wq.shape[2], 0)])
    no_gain = jnp.zeros((8, HEAD_DIM), F32)

    for layer in range(n_layers):
        last = layer == n_layers - 1
        i = layer // 2
        sh1, sc1, g1, sh2, sc2, g2 = (mod_vec(layer, k) for k in range(6))

        if layer % 2 == 0:
            p = _projection(xs, sc1, sh1, ab_w_in[i].astype(BF16), ab_flags, cos, sin_signed, no_gain,
                            out_dtype=BF16, **geo)
            ka0, va0 = a_q, a_q + a_kv
            qb0 = a_q + 2 * a_kv
            kb0, vb0 = qb0 + b_w, qb0 + 2 * b_w
            sink = a_sink[i].astype(F32) * LOG2E
            out_a = _window_attention(p, sink, n_batch=n_batch, seq_len=seq_len, ctx_len=ctx_len,
                                      k_col0=ka0, v_col0=va0)
            out_b = _neighborhood_attention(p, _neighborhood_bias(b_rpb[i]), n_batch=n_batch, seq_len=seq_len,
                                            ctx_len=ctx_len, q_col0=qb0, k_col0=kb0, v_col0=vb0)
            ctx_a = _ctx_attention(p, sink, n_batch=n_batch, ctx_row0=n_lat, ctx_len=ctx_len, q_col0=0,
                                   kv_heads=A_KV_HEADS, group=A_GROUP, k_col0=ka0, v_col0=va0)
            ctx_b = _ctx_attention(p, None, n_batch=n_batch, ctx_row0=n_lat, ctx_len=ctx_len, q_col0=qb0,
                                   kv_heads=B_HEADS, group=1, k_col0=kb0, v_col0=vb0)
            w_out = ab_w_out[i].astype(BF16)
            mm = [(out_a, ctx_a, w_out[:a_q]), (out_b, ctx_b, w_out[a_q:])]
        else:
            gains = jnp.concatenate([c_q_gain[i][None], c_k_gain[i][None], jnp.zeros((6, HEAD_DIM), F32)], axis=0)
            p = _projection(xs, sc1, sh1, c_w_in[i].astype(BF16), c_flags, cos, sin_signed, gains,
                            out_dtype=BF16, **geo)
            kc0, vc0 = c_q, c_q + c_kv
            v_all = p[:, vc0:vc0 + c_kv]
            vt_lat = jnp.swapaxes(v_all[:n_lat].reshape(n_lat // DENSE_TK, DENSE_TK, c_kv), 1, 2)
            vt_ctx = jnp.swapaxes(v_all[n_lat:].reshape(n_batch, ctx_len, c_kv), 1, 2)
            out_c = _dense_attention(p, vt_lat, vt_ctx, n_batch=n_batch, seq_len=seq_len, ctx_len=ctx_len,
                                     kv_heads=C_KV_HEADS, group=C_GROUP, k_col0=kc0)
            ctx_c = _ctx_attention(p, None, n_batch=n_batch, ctx_row0=n_lat, ctx_len=ctx_len, q_col0=0,
                                   kv_heads=C_KV_HEADS, group=C_GROUP, k_col0=kc0, v_col0=vc0)
            mm = [(out_c, ctx_c, c_w_out[i].astype(BF16))]

        xs = _residual_ln(xs, g1, ln_g[layer, 0], ln_b[layer, 0], mm=mm, **geo)

        q, h = _projection(xs, sc2, sh2, peer_wq[layer].astype(BF16), plain_flags, cos, sin_signed, no_gain,
                           out_dtype=F32, emit_h=True, **geo)
        s1, e1, s2, e2, thr = _peer_select(q, peer_subkeys[layer].astype(BF16))
        y = _peer_experts(h, peer_u[layer].astype(BF16), peer_v[layer].astype(BF16).T, s1, e1, s2, e2, thr)
        xs = _residual_ln(xs, g2, ln_g[layer, 1], ln_b[layer, 1], y=y,
                          n_out_rows=n_lat if last else None, **geo)

    return xs.reshape(n_batch, seq_len, d)
```

```python
import functools

import numpy as np
import jax
import jax.numpy as jnp
from jax import lax
from jax.experimental import pallas as pl
from jax.experimental.pallas import tpu as pltpu

F32 = jnp.float32
BF16 = jnp.bfloat16

DEPTH = 4
GRID_W = 64
HEAD_DIM = 128
ATTN_SCALE = HEAD_DIM ** -0.5
LOG2E = 1.4426950408889634
Q_SCALE = ATTN_SCALE * LOG2E
ROPE_THETA = 10000.0
NEG_INF = -1e30

A_Q_HEADS = 8
A_KV_HEADS = 2
A_GROUP = A_Q_HEADS // A_KV_HEADS
A_WINDOW = 128
B_HEADS = 8
B_WIN_ROWS = 8
B_WIN_COLS = 16
NA_ROWS = 4
C_Q_HEADS = 16
C_KV_HEADS = 4
C_GROUP = C_Q_HEADS // C_KV_HEADS

PEER_HEADS = 8
PEER_N_KEYS = 128
PEER_TOPK = 16
GATE_ROWS = 128
DENSE_TK = 512
CHUNKS_PER_TRIP = 4

DEEPNORM_ALPHA = (2 * DEPTH) ** 0.25

VMEM_LIMIT_BYTES = 56 * 1024 * 1024

ROW_TILE = 512
COL_TILES = (1536, 1024, 512, 256)
FLAG_GROUP = 256


def _dot(a, b):
    return jnp.dot(a, b, preferred_element_type=F32)


def _dot_nt(a, b):
    return lax.dot_general(a, b, (((1,), (1,)), ((), ())), preferred_element_type=F32)


def _params(sem, vmem=None):
    return pltpu.CompilerParams(dimension_semantics=sem, vmem_limit_bytes=vmem or VMEM_LIMIT_BYTES)


def _mod_kernel(c_ref, w_ref, b_ref, o_ref):
    c = c_ref[...]
    a = c * (1.0 / (1.0 + jnp.exp(-c)))
    o_ref[0] = _dot(a.astype(BF16), w_ref[0].astype(BF16)) + b_ref[0]


def _modulation(c_rows, mod_w, mod_b):
    n_layers, d, n = mod_w.shape
    tn = 1024
    rows = c_rows.shape[0]
    return pl.pallas_call(
        _mod_kernel,
        grid=(n_layers, n // tn),
        in_specs=[
            pl.BlockSpec((rows, d), lambda l, j: (0, 0)),
            pl.BlockSpec((1, d, tn), lambda l, j: (l, 0, j)),
            pl.BlockSpec((1, 1, tn), lambda l, j: (l, 0, j)),
        ],
        out_specs=pl.BlockSpec((1, rows, tn), lambda l, j: (l, 0, j)),
        out_shape=jax.ShapeDtypeStruct((n_layers, rows, n), F32),
        compiler_params=_params(("arbitrary", "arbitrary")),
    )(c_rows, mod_w, mod_b.reshape(n_layers, 1, n))


FLAG_RMS_Q = 1
FLAG_RMS_K = 2
FLAG_ROPE = 4
FLAG_SCALE = 8


def _proj_kernel(flags_ref, x_ref, sc_ref, sh_ref, w_ref, cos_ref, sin_ref, gain_ref, o_ref, *rest,
                 n_lat_tiles, emit_h):
    if emit_h:
        h_out_ref, h_scr, acc_scr = rest
    else:
        h_scr, acc_scr = rest
    i = pl.program_id(0)
    j = pl.program_id(1)
    tn = o_ref.shape[1]

    @pl.when(j == 0)
    def _():
        h = (x_ref[...] * (1.0 + sc_ref[0]) + sh_ref[0]).astype(BF16)
        h_scr[...] = h
        if emit_h:
            h_out_ref[...] = h

    acc_scr[...] = _dot(h_scr[...], w_ref[...])
    is_lat = i < n_lat_tiles
    lane = lax.broadcasted_iota(jnp.int32, (1, HEAD_DIM), 1)
    odd_seg = ((lane // (HEAD_DIM // 4)) % 2) == 1

    for g in range(tn // FLAG_GROUP):
        fl = flags_ref[j * (tn // FLAG_GROUP) + g]
        rms = fl & 3
        do_rope = jnp.logical_and((fl & FLAG_ROPE) != 0, is_lat)
        scale = jnp.where((fl & FLAG_SCALE) != 0, Q_SCALE, 1.0).astype(F32)
        for hh in range(FLAG_GROUP // HEAD_DIM):
            c0 = g * FLAG_GROUP + hh * HEAD_DIM
            cols = slice(c0, c0 + HEAD_DIM)

            @pl.when(rms != 0)
            def _():
                y = acc_scr[:, cols]
                gain = gain_ref[pl.ds(rms - 1, 1), :]
                y = y * lax.rsqrt(jnp.mean(y * y, axis=-1, keepdims=True) + 1e-6)
                acc_scr[:, cols] = y * gain

            @pl.when(do_rope)
            def _():
                y = acc_scr[:, cols]
                r_dn = pltpu.roll(y, HEAD_DIM // 4, 1)
                r_up = pltpu.roll(y, 3 * HEAD_DIM // 4, 1)
                y = y * cos_ref[...] + jnp.where(odd_seg, r_dn, r_up) * sin_ref[...]
                o_ref[:, cols] = (y * scale).astype(o_ref.dtype)

            @pl.when(jnp.logical_not(do_rope))
            def _():
                o_ref[:, cols] = (acc_scr[:, cols] * scale).astype(o_ref.dtype)


def _projection(x, sc, sh, w, flags, cos, sin_signed, gains, *, seq_len, n_lat_rows, out_dtype, emit_h=False):
    t, d = x.shape
    n = w.shape[1]
    tm = ROW_TILE
    tn = next(c for c in COL_TILES if n % c == 0)
    n_lat_tiles = n_lat_rows // tm
    pos_tiles = seq_len // tm
    n_batch = n_lat_rows // seq_len

    def mod_map(i, j, fl):
        return (jnp.minimum((i * tm) // seq_len, n_batch), 0, 0)

    out_shape = [jax.ShapeDtypeStruct((t, n), out_dtype)]
    out_specs = [pl.BlockSpec((tm, tn), lambda i, j, fl: (i, j))]
    if emit_h:
        out_shape.append(jax.ShapeDtypeStruct((t, d), BF16))
        out_specs.append(pl.BlockSpec((tm, d), lambda i, j, fl: (i, 0)))
    grid_spec = pltpu.PrefetchScalarGridSpec(
        num_scalar_prefetch=1,
        grid=(t // tm, n // tn),
        in_specs=[
            pl.BlockSpec((tm, d), lambda i, j, fl: (i, 0)),
            pl.BlockSpec((1, 1, d), mod_map),
            pl.BlockSpec((1, 1, d), mod_map),
            pl.BlockSpec((d, tn), lambda i, j, fl: (0, j)),
            pl.BlockSpec((tm, HEAD_DIM), lambda i, j, fl: (i % pos_tiles, 0)),
            pl.BlockSpec((tm, HEAD_DIM), lambda i, j, fl: (i % pos_tiles, 0)),
            pl.BlockSpec((8, HEAD_DIM), lambda i, j, fl: (0, 0)),
        ],
        out_specs=out_specs,
        scratch_shapes=[pltpu.VMEM((tm, d), BF16), pltpu.VMEM((tm, tn), F32)],
    )
    res = pl.pallas_call(
        functools.partial(_proj_kernel, n_lat_tiles=n_lat_tiles, emit_h=emit_h),
        grid_spec=grid_spec,
        out_shape=out_shape,
        compiler_params=_params(("arbitrary", "arbitrary")),
    )(flags, x, sc, sh, w, cos, sin_signed, gains)
    return res if emit_h else res[0]


def _ctx_attn_kernel(sink_ref, q_ref, k_ref, v_ref, o_ref, *, group, has_sink):
    tq = q_ref.shape[0]
    q = jnp.concatenate([q_ref[:, hh * HEAD_DIM:(hh + 1) * HEAD_DIM] for hh in range(group)], axis=0)
    s = _dot_nt(q, k_ref[...])
    m = jnp.max(s, axis=-1, keepdims=True)
    if has_sink:
        kvh = pl.program_id(1)
        sk = jnp.concatenate(
            [jnp.full((tq, 1), sink_ref[kvh * group + hh], F32) for hh in range(group)], axis=0)
        m = jnp.maximum(m, sk)
    p = jnp.exp2(s - m)
    l = jnp.sum(p, axis=-1, keepdims=True)
    if has_sink:
        l = l + jnp.exp2(sk - m)
    o = _dot(p.astype(BF16), v_ref[...]) / l
    for hh in range(group):
        o_ref[:, hh * HEAD_DIM:(hh + 1) * HEAD_DIM] = o[hh * tq:(hh + 1) * tq].astype(o_ref.dtype)


def _ctx_attention(p, sink, *, n_batch, ctx_row0, ctx_len, q_col0, kv_heads, group, k_col0, v_col0):
    has_sink = sink is not None
    if sink is None:
        sink = jnp.zeros((kv_heads * group,), F32)
    gw = group * HEAD_DIM
    kb, vb, qb = k_col0 // HEAD_DIM, v_col0 // HEAD_DIM, q_col0 // gw
    c0b = ctx_row0 // ctx_len
    return pl.pallas_call(
        functools.partial(_ctx_attn_kernel, group=group, has_sink=has_sink),
        grid=(n_batch, kv_heads),
        in_specs=[
            pl.BlockSpec(memory_space=pltpu.SMEM),
            pl.BlockSpec((ctx_len, gw), lambda b, h: (c0b + b, qb + h)),
            pl.BlockSpec((ctx_len, HEAD_DIM), lambda b, h: (c0b + b, kb + h)),
            pl.BlockSpec((ctx_len, HEAD_DIM), lambda b, h: (c0b + b, vb + h)),
        ],
        out_specs=pl.BlockSpec((ctx_len, gw), lambda b, h: (b, h)),
        out_shape=jax.ShapeDtypeStruct((n_batch * ctx_len, kv_heads * gw), BF16),
        compiler_params=_params(("arbitrary", "arbitrary")),
    )(sink, p, p, p)


def _dense_attn_kernel(q_ref, kc_ref, vtc_ref, kl_ref, vtl_ref, o_ref, m_scr, l_scr, acc_scr, *, group, lanes,
                       per_trip):
    tq = q_ref.shape[0]
    n_q = group * tq
    n_chunks = vtl_ref.shape[0]
    tk = vtl_ref.shape[2]
    q = jnp.concatenate([q_ref[:, hh * HEAD_DIM:(hh + 1) * HEAD_DIM] for hh in range(group)], axis=0)
    q_groups = [q[g * lanes:(g + 1) * lanes] for g in range(n_q // lanes)]

    def softmax_step(state, s, vt):
        mx = jnp.max(s, axis=0, keepdims=True)
        if state is None:
            m_new = mx
        else:
            m_old, l_old, acc_old = state
            m_new = jnp.maximum(m_old, mx)
            a = jnp.exp2(m_old - m_new)
        p = jnp.exp2(s - m_new)
        psum = jnp.sum(p, axis=0, keepdims=True)
        pv = _dot(vt, p.astype(BF16))
        if state is None:
            return m_new, psum, pv
        return m_new, a * l_old + psum, a * acc_old + pv

    def update(chunks, first):
        scores = [[_dot_nt(k, qg) for qg in q_groups] for k, _ in chunks]
        states = []
        for g in range(len(q_groups)):
            cols = slice(g * lanes, (g + 1) * lanes)
            states.append(None if first else (m_scr[:, cols], l_scr[:, cols], acc_scr[:, cols]))
        for ci, (_, vt) in enumerate(chunks):
            states = [softmax_step(states[g], scores[ci][g], vt) for g in range(len(q_groups))]
        for g, (m, l, acc) in enumerate(states):
            cols = slice(g * lanes, (g + 1) * lanes)
            m_scr[:, cols] = m
            l_scr[:, cols] = l
            acc_scr[:, cols] = acc

    update([(kc_ref[...], vtc_ref[0])], True)

    def body(c, carry):
        chunks = []
        for half in range(per_trip):
            cc = per_trip * c + half
            r0 = pl.multiple_of(cc * tk, tk)
            chunks.append((kl_ref[pl.ds(r0, tk), :], vtl_ref[cc]))
        update(chunks, False)
        return carry

    lax.fori_loop(0, n_chunks // per_trip, body, 0)

    o = (acc_scr[...] / l_scr[...]).T
    for hh in range(group):
        o_ref[:, hh * HEAD_DIM:(hh + 1) * HEAD_DIM] = o[hh * tq:(hh + 1) * tq].astype(o_ref.dtype)


def _dense_attention(p, vt_lat, vt_ctx, *, n_batch, seq_len, ctx_len, kv_heads, group, k_col0, tq=512, lanes=256):
    nq = seq_len // tq
    gw = group * HEAD_DIM
    kb = k_col0 // HEAD_DIM
    c0b = n_batch * seq_len // ctx_len
    n_chunks = vt_lat.shape[0] // n_batch
    tk = vt_lat.shape[2]
    n_q = group * tq
    return pl.pallas_call(
        functools.partial(_dense_attn_kernel, group=group, lanes=lanes,
                          per_trip=next(c for c in (CHUNKS_PER_TRIP, 2, 1) if n_chunks % c == 0)),
        grid=(n_batch, kv_heads, nq),
        in_specs=[
            pl.BlockSpec((tq, gw), lambda b, h, i: (b * nq + i, h)),
            pl.BlockSpec((ctx_len, HEAD_DIM), lambda b, h, i: (c0b + b, kb + h)),
            pl.BlockSpec((1, HEAD_DIM, ctx_len), lambda b, h, i: (b, h, 0)),
            pl.BlockSpec((seq_len, HEAD_DIM), lambda b, h, i: (b, kb + h)),
            pl.BlockSpec((n_chunks, HEAD_DIM, tk), lambda b, h, i: (b, h, 0)),
        ],
        out_specs=pl.BlockSpec((tq, gw), lambda b, h, i: (b * nq + i, h)),
        out_shape=jax.ShapeDtypeStruct((n_batch * seq_len, kv_heads * gw), BF16),
        scratch_shapes=[pltpu.VMEM((1, n_q), F32), pltpu.VMEM((1, n_q), F32), pltpu.VMEM((HEAD_DIM, n_q), F32)],
        compiler_params=_params(("arbitrary", "arbitrary", "arbitrary")),
    )(p, p, vt_ctx, p, vt_lat)


def _window_kernel(sink_ref, q_ref, k_ref, v_ref, kc_ref, vc_ref, o_ref, *, seq_len):
    kvh = pl.program_id(1)
    iq = pl.program_id(2)
    tq = q_ref.shape[0]
    win = tq + 2 * A_WINDOW
    q0 = iq * tq
    start = pl.multiple_of(jnp.clip(q0 - A_WINDOW, 0, seq_len - win), A_WINDOW)
    q = jnp.concatenate([q_ref[:, hh * HEAD_DIM:(hh + 1) * HEAD_DIM] for hh in range(A_GROUP)], axis=0)
    vw = v_ref[pl.ds(start, win), :]

    s_all = _dot_nt(q, k_ref[pl.ds(start, win), :])
    sc_all = _dot_nt(q, kc_ref[...])
    qpos = q0 + lax.broadcasted_iota(jnp.int32, (tq, win), 0)
    kpos = start + lax.broadcasted_iota(jnp.int32, (tq, win), 1)
    in_window = jnp.abs(kpos - qpos) <= A_WINDOW
    for hh in range(A_GROUP):
        rows = slice(hh * tq, (hh + 1) * tq)
        s = jnp.where(in_window, s_all[rows], NEG_INF)
        sc = sc_all[rows]
        sk = sink_ref[kvh * A_GROUP + hh]
        m = jnp.maximum(jnp.maximum(jnp.max(s, axis=-1, keepdims=True), jnp.max(sc, axis=-1, keepdims=True)), sk)
        p = jnp.exp2(s - m)
        pc = jnp.exp2(sc - m)
        l = jnp.sum(p, axis=-1, keepdims=True) + jnp.sum(pc, axis=-1, keepdims=True) + jnp.exp2(sk - m)
        o = (_dot(p.astype(BF16), vw) + _dot(pc.astype(BF16), vc_ref[...])) / l
        o_ref[:, hh * HEAD_DIM:(hh + 1) * HEAD_DIM] = o.astype(o_ref.dtype)


def _window_attention(p, sink, *, n_batch, seq_len, ctx_len, k_col0, v_col0, tq=128):
    nq = seq_len // tq
    gw = A_GROUP * HEAD_DIM
    kb, vb = k_col0 // HEAD_DIM, v_col0 // HEAD_DIM
    c0b = n_batch * seq_len // ctx_len
    return pl.pallas_call(
        functools.partial(_window_kernel, seq_len=seq_len),
        grid=(n_batch, A_KV_HEADS, nq),
        in_specs=[
            pl.BlockSpec(memory_space=pltpu.SMEM),
            pl.BlockSpec((tq, gw), lambda b, h, i: (b * nq + i, h)),
            pl.BlockSpec((seq_len, HEAD_DIM), lambda b, h, i: (b, kb + h)),
            pl.BlockSpec((seq_len, HEAD_DIM), lambda b, h, i: (b, vb + h)),
            pl.BlockSpec((ctx_len, HEAD_DIM), lambda b, h, i: (c0b + b, kb + h)),
            pl.BlockSpec((ctx_len, HEAD_DIM), lambda b, h, i: (c0b + b, vb + h)),
        ],
        out_specs=pl.BlockSpec((tq, gw), lambda b, h, i: (b * nq + i, h)),
        out_shape=jax.ShapeDtypeStruct((n_batch * seq_len, A_KV_HEADS * gw), BF16),
        compiler_params=_params(("arbitrary", "arbitrary", "arbitrary")),
    )(sink, p, p, p, p, p)


def _neighborhood_kernel(q_ref, k_ref, v_ref, kc_ref, vc_ref, bias_ref, o_ref, *, n_rows):
    n_keys = B_WIN_ROWS * GRID_W

    def body(t, carry):
        q0 = pl.multiple_of(t * (NA_ROWS * GRID_W), NA_ROWS * GRID_W)
        q_all = q_ref[pl.ds(q0, NA_ROWS * GRID_W), :]
        sc_all = _dot_nt(q_all, kc_ref[...])
        local = []
        for j in range(NA_ROWS):
            r = t * NA_ROWS + j
            rstart = jnp.clip(r - B_WIN_ROWS // 2, 0, n_rows - B_WIN_ROWS)
            k0 = pl.multiple_of(rstart * GRID_W, GRID_W)
            s = _dot_nt(q_all[j * GRID_W:(j + 1) * GRID_W], k_ref[pl.ds(k0, n_keys), :]) + bias_ref[0, r - rstart]
            local.append((k0, s))
        outs = []
        for j, (k0, s) in enumerate(local):
            sc = sc_all[j * GRID_W:(j + 1) * GRID_W]
            m = jnp.maximum(jnp.max(s, axis=-1, keepdims=True), jnp.max(sc, axis=-1, keepdims=True))
            p = jnp.exp2(s - m)
            pc = jnp.exp2(sc - m)
            l = jnp.sum(p, axis=-1, keepdims=True) + jnp.sum(pc, axis=-1, keepdims=True)
            o = (_dot(p.astype(BF16), v_ref[pl.ds(k0, n_keys), :]) + _dot(pc.astype(BF16), vc_ref[...])) / l
            outs.append(o.astype(o_ref.dtype))
        o_ref[pl.ds(q0, NA_ROWS * GRID_W), :] = jnp.concatenate(outs, axis=0)
        return carry

    lax.fori_loop(0, n_rows // NA_ROWS, body, 0)


def _neighborhood_bias(rpb):
    qc = np.arange(GRID_W)[:, None]
    kc = np.arange(GRID_W)[None, :]
    wstart = np.clip(qc - B_WIN_COLS // 2, 0, GRID_W - B_WIN_COLS)
    col_ok = (kc >= wstart) & (kc < wstart + B_WIN_COLS)
    dcol = np.clip(kc - qc + B_WIN_COLS - 1, 0, 2 * B_WIN_COLS - 2)
    d = np.arange(B_WIN_ROWS)[:, None]
    kk = np.arange(B_WIN_ROWS)[None, :]
    drow = kk - d + B_WIN_ROWS - 1
    bias = (rpb.astype(F32) * LOG2E)[:, drow][:, :, :, dcol]
    bias = jnp.where(col_ok[None, None, None], bias, NEG_INF)
    bias = jnp.transpose(bias, (0, 1, 3, 2, 4))
    return bias.reshape(rpb.shape[0], B_WIN_ROWS, GRID_W, B_WIN_ROWS * GRID_W)


def _neighborhood_attention(p, bias, *, n_batch, seq_len, ctx_len, q_col0, k_col0, v_col0):
    qb, kb, vb = q_col0 // HEAD_DIM, k_col0 // HEAD_DIM, v_col0 // HEAD_DIM
    c0b = n_batch * seq_len // ctx_len
    n_rows = seq_len // GRID_W
    return pl.pallas_call(
        functools.partial(_neighborhood_kernel, n_rows=n_rows),
        grid=(n_batch, B_HEADS),
        in_specs=[
            pl.BlockSpec((seq_len, HEAD_DIM), lambda b, h: (b, qb + h)),
            pl.BlockSpec((seq_len, HEAD_DIM), lambda b, h: (b, kb + h)),
            pl.BlockSpec((seq_len, HEAD_DIM), lambda b, h: (b, vb + h)),
            pl.BlockSpec((ctx_len, HEAD_DIM), lambda b, h: (c0b + b, kb + h)),
            pl.BlockSpec((ctx_len, HEAD_DIM), lambda b, h: (c0b + b, vb + h)),
            pl.BlockSpec((1, B_WIN_ROWS, GRID_W, B_WIN_ROWS * GRID_W), lambda b, h: (h, 0, 0, 0)),
        ],
        out_specs=pl.BlockSpec((seq_len, HEAD_DIM), lambda b, h: (b, h)),
        out_shape=jax.ShapeDtypeStruct((n_batch * seq_len, B_HEADS * HEAD_DIM), BF16),
        compiler_params=_params(("arbitrary", "arbitrary")),
    )(p, p, p, p, p, bias)


def _residual_ln_kernel(*refs, n_mm, n_lat_tiles):
    if n_mm:
        lat_refs = refs[:n_mm]
        ctx_refs = refs[n_mm:2 * n_mm]
        w_refs = refs[2 * n_mm:3 * n_mm]
        x_ref, g_ref, lg_ref, lb_ref, o_ref = refs[3 * n_mm:]
        is_lat = pl.program_id(0) < n_lat_tiles
        y = None
        for lat_ref, ctx_ref, w_ref in zip(lat_refs, ctx_refs, w_refs):
            a = jnp.where(is_lat, lat_ref[...], ctx_ref[...])
            yi = _dot(a, w_ref[...])
            y = yi if y is None else y + yi
    else:
        y_ref, x_ref, g_ref, lg_ref, lb_ref, o_ref = refs
        y = y_ref[...]
    z = DEEPNORM_ALPHA * x_ref[...] + g_ref[0] * y
    mu = jnp.mean(z, axis=-1, keepdims=True)
    zc = z - mu
    var = jnp.mean(zc * zc, axis=-1, keepdims=True)
    o_ref[...] = (zc * lax.rsqrt(var + 1e-5)) * lg_ref[...] + lb_ref[...]


def _residual_ln(x, gate, ln_g, ln_b, *, seq_len, n_lat_rows, mm=(), y=None, n_out_rows=None, tm=256):
    t, d = x.shape
    n_out_rows = n_out_rows or t
    n_batch = n_lat_rows // seq_len

    def mod_map(i):
        return (jnp.minimum((i * tm) // seq_len, n_batch), 0, 0)

    n_lat_tiles = n_lat_rows // tm
    n_ctx_tiles = (t - n_lat_rows) // tm
    in_specs, args = [], []
    for a_lat, _, _ in mm:
        in_specs.append(pl.BlockSpec((tm, a_lat.shape[1]), lambda i: (jnp.minimum(i, n_lat_tiles - 1), 0)))
        args.append(a_lat)
    for _, a_ctx, _ in mm:
        in_specs.append(pl.BlockSpec((tm, a_ctx.shape[1]),
                                     lambda i: (jnp.clip(i - n_lat_tiles, 0, n_ctx_tiles - 1), 0)))
        args.append(a_ctx)
    for _, _, w in mm:
        in_specs.append(pl.BlockSpec(w.shape, lambda i: (0, 0)))
        args.append(w)
    if not mm:
        in_specs.append(pl.BlockSpec((tm, d), lambda i: (i, 0)))
        args.append(y)
    in_specs += [
        pl.BlockSpec((tm, d), lambda i: (i, 0)),
        pl.BlockSpec((1, 1, d), mod_map),
        pl.BlockSpec((1, d), lambda i: (0, 0)),
        pl.BlockSpec((1, d), lambda i: (0, 0)),
    ]
    args += [x, gate, ln_g.reshape(1, d), ln_b.reshape(1, d)]
    return pl.pallas_call(
        functools.partial(_residual_ln_kernel, n_mm=len(mm), n_lat_tiles=n_lat_rows // tm),
        grid=(n_out_rows // tm,),
        in_specs=in_specs,
        out_specs=pl.BlockSpec((tm, d), lambda i: (i, 0)),
        out_shape=jax.ShapeDtypeStruct((n_out_rows, d), F32),
        compiler_params=_params(("arbitrary",)),
    )(*args)


def _top_values(xs, out_scrs):
    xs = list(xs)
    ridx = [lax.broadcasted_iota(jnp.int32, x.shape, 0).astype(F32) for x in xs]
    for k in range(PEER_TOPK):
        for n, out_scr in enumerate(out_scrs):
            x = xs[n]
            m = jnp.max(x, axis=0, keepdims=True)
            out_scr[k:k + 1, :] = m
            first = jnp.min(jnp.where(x == m, ridx[n], float(x.shape[0])), axis=0, keepdims=True)
            xs[n] = jnp.where(ridx[n] == first, -jnp.inf, x)


def _select_kernel(q_ref, sk_ref, s1_ref, e1_ref, s2_ref, e2_ref, thr_ref, t1_scr, t2_scr, cand_scr, best_scr):
    n_heads = sk_ref.shape[0]
    q = q_ref[...].astype(BF16)
    s1, s2 = [], []
    for h in range(n_heads):
        c0 = 2 * h * HEAD_DIM
        s1.append(_dot_nt(sk_ref[h, 0], q[:, c0:c0 + HEAD_DIM]))
        s2.append(_dot_nt(sk_ref[h, 1], q[:, c0 + HEAD_DIM:c0 + 2 * HEAD_DIM]))
    _top_values(s1 + s2, [t1_scr.at[h] for h in range(n_heads)] + [t2_scr.at[h] for h in range(n_heads)])
    sub = lax.broadcasted_iota(jnp.int32, (8, 1), 0)
    for h in range(n_heads):
        t1, t2, cand = t1_scr.at[h], t2_scr.at[h], cand_scr.at[h]
        cand[0:16, :] = t1[0:1, :] + t2[0:16, :]
        for a in range(1, 8):
            c = t1[a:a + 1, :] + t2[0:8, :]
            cand[8 + 8 * a:16 + 8 * a, :] = jnp.where(sub < PEER_TOPK // (a + 1), c, -jnp.inf)
        cand[72:80, :] = t1[8:16, :] + t2[0:1, :]
    _top_values([cand_scr[h] for h in range(n_heads)], [best_scr.at[h] for h in range(n_heads)])
    for h in range(n_heads):
        best = best_scr[h]
        z = jnp.sum(jnp.exp(best - best[0:1, :]), axis=0, keepdims=True)
        s1_ref[h] = s1[h]
        s2_ref[h] = s2[h]
        e1_ref[h] = jnp.exp(s1[h] - t1_scr[h, 0:1, :]) / z
        e2_ref[h] = jnp.exp(s2[h] - t2_scr[h, 0:1, :])
        thr_ref[h] = best[PEER_TOPK - 1:PEER_TOPK, :]


def _peer_select(q, subkeys, tt=256, heads_per_step=4):
    t = q.shape[0]
    nk = PEER_N_KEYS
    hps = heads_per_step
    big = jax.ShapeDtypeStruct((PEER_HEADS, nk, t), F32)
    big_spec = pl.BlockSpec((hps, nk, tt), lambda i, h: (h, 0, i))
    return pl.pallas_call(
        _select_kernel,
        grid=(t // tt, PEER_HEADS // hps),
        in_specs=[
            pl.BlockSpec((tt, hps * 2 * HEAD_DIM), lambda i, h: (i, h)),
            pl.BlockSpec((hps, 2, nk, HEAD_DIM), lambda i, h: (h, 0, 0, 0)),
        ],
        out_specs=[big_spec, big_spec, big_spec, big_spec, pl.BlockSpec((hps, 1, tt), lambda i, h: (h, 0, i))],
        out_shape=[big, big, big, big, jax.ShapeDtypeStruct((PEER_HEADS, 1, t), F32)],
        scratch_shapes=[pltpu.VMEM((hps, PEER_TOPK, tt), F32), pltpu.VMEM((hps, PEER_TOPK, tt), F32),
                        pltpu.VMEM((hps, 80, tt), F32), pltpu.VMEM((hps, PEER_TOPK, tt), F32)],
        compiler_params=_params(("arbitrary", "arbitrary")),
    )(q, subkeys)


def _peer_kernel(h_ref, u_ref, vt_ref, s1a_ref, e1a_ref, s1b_ref, e1b_ref, s2_ref, e2_ref, thr_ref, o_ref,
                 a0_scr, a1_scr, w0_scr, w1_scr, acc_scr):
    k = pl.program_id(1)
    te, tt = a0_scr.shape
    nk = PEER_N_KEYS
    ni = te // nk

    @pl.when(k == 0)
    def _():
        acc_scr[...] = jnp.zeros_like(acc_scr)
        a1_scr[...] = jnp.zeros_like(a1_scr)
        w0_scr[...] = jnp.zeros_like(w0_scr)
        w1_scr[...] = jnp.zeros_like(w1_scr)

    def gate_stage(a_scr, w_scr, s1_ref, e1_ref, row0):
        for tg in range(tt // 128):
            lanes = slice(tg * 128, (tg + 1) * 128)
            for jb in range(nk // GATE_ROWS):
                keys = slice(jb * GATE_ROWS, (jb + 1) * GATE_ROWS)
                gates = [None] * ni
                for hd in range(PEER_HEADS):
                    s2 = s2_ref[hd, keys, lanes]
                    e2 = e2_ref[hd, keys, lanes]
                    thr = thr_ref[hd, :, lanes]
                    for il in range(ni):
                        r = row0 + il
                        cand = s1_ref[hd, r:r + 1, lanes] + s2
                        sel = jnp.where(cand >= thr, e1_ref[hd, r:r + 1, lanes] * e2, 0.0)
                        gates[il] = sel if gates[il] is None else gates[il] + sel
                for il in range(ni):
                    rows = slice(il * nk + jb * GATE_ROWS, il * nk + (jb + 1) * GATE_ROWS)
                    a = a_scr[rows, lanes]
                    act = 0.5 * a * (1.0 + lax.erf(a * (2.0 ** -0.5)))
                    w_scr[rows, lanes] = (gates[il] * act).astype(BF16)

    def stage(u, vt, a_new, a_old, w_new, w_old, s1_ref, e1_ref, row0):
        a_new[...] = _dot_nt(u, h_ref[...])
        y = _dot(vt, w_old[...])
        gate_stage(a_old, w_new, s1_ref, e1_ref, row0)
        return y

    ya = stage(u_ref[:te, :], vt_ref[:, :te], a0_scr, a1_scr, w1_scr, w0_scr, s1a_ref, e1a_ref, ni)
    yb = stage(u_ref[te:, :], vt_ref[:, te:], a1_scr, a0_scr, w0_scr, w1_scr, s1b_ref, e1b_ref, 0)
    acc_scr[...] += ya + yb

    @pl.when(k == pl.num_programs(1) - 1)
    def _():
        o_ref[...] = acc_scr[...].T


PEER_TE = 512


def _peer_experts(h, u, vt, s1, e1, s2, e2, thr, *, tt=512):
    t, d = h.shape
    n_exp = u.shape[0]
    nk = PEER_N_KEYS
    te = PEER_TE
    nb = n_exp // (2 * te)
    last = nb - 1

    def row_spec(shift):
        return pl.BlockSpec((PEER_HEADS, 2 * te // nk, tt),
                            lambda i, k: (0, jnp.clip(k + shift, 0, last), i))

    col_spec = pl.BlockSpec((PEER_HEADS, nk, tt), lambda i, k: (0, 0, i))
    return pl.pallas_call(
        _peer_kernel,
        grid=(t // tt, nb + 1),
        in_specs=[
            pl.BlockSpec((tt, d), lambda i, k: (i, 0)),
            pl.BlockSpec((2 * te, d), lambda i, k: (jnp.minimum(k, last), 0)),
            pl.BlockSpec((d, 2 * te), lambda i, k: (0, jnp.maximum(k - 1, 0))),
            row_spec(-1), row_spec(-1), row_spec(0), row_spec(0), col_spec, col_spec,
            pl.BlockSpec((PEER_HEADS, 1, tt), lambda i, k: (0, 0, i)),
        ],
        out_specs=pl.BlockSpec((tt, d), lambda i, k: (i, 0)),
        out_shape=jax.ShapeDtypeStruct((t, d), F32),
        scratch_shapes=[pltpu.VMEM((te, tt), F32), pltpu.VMEM((te, tt), F32),
                        pltpu.VMEM((te, tt), BF16), pltpu.VMEM((te, tt), BF16), pltpu.VMEM((d, tt), F32)],
        compiler_params=_params(("arbitrary", "arbitrary")),
    )(h, u, vt, s1, e1, s1, e1, s2, e2, thr)


def _rope_tables(seq_len):
    t = jnp.arange(seq_len, dtype=jnp.int32)
    row = (t // GRID_W).astype(F32)
    col = (t % GRID_W).astype(F32)
    n_freq = HEAD_DIM // 4
    inv_freq = ROPE_THETA ** (-jnp.arange(n_freq, dtype=F32) / n_freq)
    ang_r = row[:, None] * inv_freq[None, :]
    ang_c = col[:, None] * inv_freq[None, :]
    ang = jnp.concatenate([ang_r, ang_r, ang_c, ang_c], axis=-1)
    sign = np.tile(np.repeat(np.array([-1.0, 1.0], np.float32), n_freq), 2)
    return jnp.cos(ang), jnp.sin(ang) * sign


def _col_flags(sizes_and_flags):
    out = []
    for size, fl in sizes_and_flags:
        out += [fl] * (size // FLAG_GROUP)
    return jnp.asarray(out, jnp.int32)


def kernel(x, c, ctx, c_ctx, mod_w, mod_b, ln_g, ln_b, ab_w_in, ab_w_out, a_sink, b_rpb, c_w_in, c_w_out,
           c_q_gain, c_k_gain, peer_wq, peer_subkeys, peer_u, peer_v):
    n_batch, seq_len, d = x.shape
    ctx_len = ctx.shape[1]
    n_layers = mod_w.shape[0]
    n_lat = n_batch * seq_len
    n_ctx = n_batch * ctx_len
    assert seq_len % ROW_TILE == 0 and n_ctx % ROW_TILE == 0 and seq_len % GRID_W == 0
    geo = dict(seq_len=seq_len, n_lat_rows=n_lat)

    xs = jnp.concatenate([x.reshape(n_lat, d), ctx.reshape(n_ctx, d)], axis=0)
    cos, sin_signed = _rope_tables(seq_len)

    c_rows = jnp.concatenate([c, c_ctx[None, :], jnp.zeros((8 - n_batch - 1, d), F32)], axis=0)
    mod = _modulation(c_rows, mod_w, mod_b).reshape(n_layers, 8, 6, d)

    def mod_vec(layer, k):
        return mod[layer, :n_batch + 1, k, :].reshape(n_batch + 1, 1, d)

    a_q, a_kv, b_w = A_Q_HEADS * HEAD_DIM, A_KV_HEADS * HEAD_DIM, B_HEADS * HEAD_DIM
    ab_flags = _col_flags([(a_q, FLAG_ROPE | FLAG_SCALE), (a_kv, FLAG_ROPE), (a_kv, 0),
                           (b_w, FLAG_SCALE), (b_w, 0), (b_w, 0)])
    c_q, c_kv = C_Q_HEADS * HEAD_DIM, C_KV_HEADS * HEAD_DIM
    c_flags = _col_flags([(c_q, FLAG_RMS_Q | FLAG_ROPE | FLAG_SCALE), (c_kv, FLAG_RMS_K | FLAG_ROPE), (c_kv, 0)])
    plain_flags = _col_flags([(peer_wq.shape[2], 0)])
    no_gain = jnp.zeros((8, HEAD_DIM), F32)

    for layer in range(n_layers):
        last = layer == n_layers - 1
        i = layer // 2
        sh1, sc1, g1, sh2, sc2, g2 = (mod_vec(layer, k) for k in range(6))

        if layer % 2 == 0:
            p = _projection(xs, sc1, sh1, ab_w_in[i].astype(BF16), ab_flags, cos, sin_signed, no_gain,
                            out_dtype=BF16, **geo)
            ka0, va0 = a_q, a_q + a_kv
            qb0 = a_q + 2 * a_kv
            kb0, vb0 = qb0 + b_w, qb0 + 2 * b_w
            sink = a_sink[i].astype(F32) * LOG2E
            out_a = _window_attention(p, sink, n_batch=n_batch, seq_len=seq_len, ctx_len=ctx_len,
                                      k_col0=ka0, v_col0=va0)
            out_b = _neighborhood_attention(p, _neighborhood_bias(b_rpb[i]), n_batch=n_batch, seq_len=seq_len,
                                            ctx_len=ctx_len, q_col0=qb0, k_col0=kb0, v_col0=vb0)
            ctx_a = _ctx_attention(p, sink, n_batch=n_batch, ctx_row0=n_lat, ctx_len=ctx_len, q_col0=0,
                                   kv_heads=A_KV_HEADS, group=A_GROUP, k_col0=ka0, v_col0=va0)
            ctx_b = _ctx_attention(p, None, n_batch=n_batch, ctx_row0=n_lat, ctx_len=ctx_len, q_col0=qb0,
                                   kv_heads=B_HEADS, group=1, k_col0=kb0, v_col0=vb0)
            w_out = ab_w_out[i].astype(BF16)
            mm = [(out_a, ctx_a, w_out[:a_q]), (out_b, ctx_b, w_out[a_q:])]
        else:
            gains = jnp.concatenate([c_q_gain[i][None], c_k_gain[i][None], jnp.zeros((6, HEAD_DIM), F32)], axis=0)
            p = _projection(xs, sc1, sh1, c_w_in[i].astype(BF16), c_flags, cos, sin_signed, gains,
                            out_dtype=BF16, **geo)
            kc0, vc0 = c_q, c_q + c_kv
            v_all = p[:, vc0:vc0 + c_kv]
            vt_lat = jnp.swapaxes(v_all[:n_lat].reshape(n_lat // DENSE_TK, DENSE_TK, c_kv), 1, 2)
            vt_ctx = jnp.swapaxes(v_all[n_lat:].reshape(n_batch, ctx_len, c_kv), 1, 2)
            out_c = _dense_attention(p, vt_lat, vt_ctx, n_batch=n_batch, seq_len=seq_len, ctx_len=ctx_len,
                                     kv_heads=C_KV_HEADS, group=C_GROUP, k_col0=kc0)
            ctx_c = _ctx_attention(p, None, n_batch=n_batch, ctx_row0=n_lat, ctx_len=ctx_len, q_col0=0,
                                   kv_heads=C_KV_HEADS, group=C_GROUP, k_col0=kc0, v_col0=vc0)
            mm = [(out_c, ctx_c, c_w_out[i].astype(BF16))]

        xs = _residual_ln(xs, g1, ln_g[layer, 0], ln_b[layer, 0], mm=mm, **geo)

        q, h = _projection(xs, sc2, sh2, peer_wq[layer].astype(BF16), plain_flags, cos, sin_signed, no_gain,
                           out_dtype=BF16, emit_h=True, **geo)
        s1, e1, s2, e2, thr = _peer_select(q, peer_subkeys[layer].astype(BF16))
        y = _peer_experts(h, peer_u[layer].astype(BF16), peer_v[layer].astype(BF16).T, s1, e1, s2, e2, thr)
        xs = _residual_ln(xs, g2, ln_g[layer, 1], ln_b[layer, 1], y=y,
                          n_out_rows=n_lat if last else None, **geo)

    return xs.reshape(n_batch, seq_len, d)
```
